```python
import math
import jax, jax.numpy as jnp
from jax import lax
import numpy as np

D_MODEL = 2048
BATCH = 4
SEQ = 4096
DEPTH = 2

NORM_EPS = 1e-6
BLOCK = 128

LRU_WIDTH = 1024
LRU_BLOCKS = 8
LRU_BLOCK_DIM = LRU_WIDTH // LRU_BLOCKS
CONV_WIDTH = 4
LRU_C = 8.0
SWA_Q_HEADS = 16
SWA_KV_HEADS = 2
SWA_HEAD_DIM = 64
SWA_GROUP = SWA_Q_HEADS // SWA_KV_HEADS
SWA_WIDTH = SWA_Q_HEADS * SWA_HEAD_DIM
SWA_KV_WIDTH = SWA_KV_HEADS * SWA_HEAD_DIM
WINDOW = 128
EVEN_SPLITS = [LRU_WIDTH, LRU_WIDTH, SWA_WIDTH, SWA_KV_WIDTH, SWA_KV_WIDTH, SWA_WIDTH]
EVEN_IN = sum(EVEN_SPLITS)
EVEN_MIX = LRU_WIDTH + SWA_WIDTH

MLA_HEADS = 8
MLA_Q_RANK = 768
MLA_KV_RANK = 512
MLA_NOPE = 128
MLA_ROPE = 64
MLA_V = 128
MLA_WIDTH = MLA_HEADS * MLA_V
ROPE_THETA = 10000.0
DIFF_HEADS = 8
DIFF_QK = 64
DIFF_V = 2 * DIFF_QK
DIFF_QK_WIDTH = DIFF_HEADS * 2 * DIFF_QK
DIFF_WIDTH = DIFF_HEADS * DIFF_V
DIFF_LAYER_IDX = 1
DIFF_LAMBDA_INIT = 0.8 - 0.6 * math.exp(-0.3 * DIFF_LAYER_IDX)
ODD_SPLITS = [MLA_Q_RANK, MLA_KV_RANK, MLA_ROPE, MLA_WIDTH,
              DIFF_QK_WIDTH, DIFF_QK_WIDTH, DIFF_WIDTH, DIFF_WIDTH]
ODD_IN = sum(ODD_SPLITS)
ODD_MIX = MLA_WIDTH + DIFF_WIDTH

kernel_name = "hybrid_rglru_swa_mla_diff_trunk"


def rmsnorm(x, g):
    xf = x.astype(jnp.float32)
    y = xf * lax.rsqrt(jnp.mean(xf * xf, axis=-1, keepdims=True) + NORM_EPS)
    return (y * g.astype(jnp.float32)).astype(x.dtype)


def split_cols(z, sizes):
    idx = [int(i) for i in np.cumsum(sizes)[:-1]]
    return jnp.split(z, idx, axis=-1)


def rope(x, pos):
    half = x.shape[-1] // 2
    freq = ROPE_THETA ** (-jnp.arange(half, dtype=jnp.float32) / half)
    ang = pos.astype(jnp.float32)[:, None] * freq[None, :]
    cos = jnp.cos(ang)[:, None, :]
    sin = jnp.sin(ang)[:, None, :]
    xf = x.astype(jnp.float32)
    x1, x2 = xf[..., :half], xf[..., half:]
    return jnp.concatenate([x1 * cos - x2 * sin, x1 * sin + x2 * cos], axis=-1).astype(x.dtype)


def causal_depthwise_conv(x, w, b):
    y = lax.conv_general_dilated(
        x, w[:, None, :].astype(x.dtype), window_strides=(1,),
        padding=[(CONV_WIDTH - 1, 0)], dimension_numbers=("NWC", "WIO", "NWC"),
        feature_group_count=x.shape[-1])
    return y + b


def rg_lru(x, gx_w, gx_b, ga_w, ga_b, lru_lambda):
    B, S, _ = x.shape
    xb = x.reshape(B, S, LRU_BLOCKS, LRU_BLOCK_DIM)
    gate_i = jax.nn.sigmoid(jnp.einsum("bsnd,nde->bsne", xb, gx_w) + gx_b).reshape(B, S, LRU_WIDTH)
    gate_r = jax.nn.sigmoid(jnp.einsum("bsnd,nde->bsne", xb, ga_w) + ga_b).reshape(B, S, LRU_WIDTH)
    log_a = -LRU_C * jax.nn.softplus(-lru_lambda.astype(jnp.float32)) * gate_r.astype(jnp.float32)
    a = jnp.exp(log_a)
    mult = jnp.sqrt(-jnp.expm1(2.0 * log_a))
    u = mult * gate_i.astype(jnp.float32) * x.astype(jnp.float32)

    def combine(c1, c2):
        a1, b1 = c1
        a2, b2 = c2
        return a1 * a2, a2 * b1 + b2

    _, h = lax.associative_scan(combine, (a, u), axis=1)
    return h.astype(x.dtype)


def sliding_window_gqa_sinks(q, k, v, sinks):
    B, S = q.shape[:2]
    nb = S // BLOCK
    qb = q.reshape(B, nb, BLOCK, SWA_KV_HEADS, SWA_GROUP, SWA_HEAD_DIM)

    def band(t):
        tp = jnp.pad(t, ((0, 0), (BLOCK, 0), (0, 0), (0, 0)))
        cur = tp[:, BLOCK:].reshape(B, nb, BLOCK, SWA_KV_HEADS, SWA_HEAD_DIM)
        prev = tp[:, :S].reshape(B, nb, BLOCK, SWA_KV_HEADS, SWA_HEAD_DIM)
        return jnp.concatenate([prev, cur], axis=2)

    kb, vb = band(k), band(v)
    s = jnp.einsum("bnqhgd,bnkhd->bnhgqk", qb, kb).astype(jnp.float32) * (SWA_HEAD_DIM ** -0.5)
    n = jnp.arange(nb)[:, None, None]
    qi = jnp.arange(BLOCK)[None, :, None]
    ki = jnp.arange(2 * BLOCK)[None, None, :]
    rel = qi + BLOCK - ki
    kpos = n * BLOCK - BLOCK + ki
    mask = (rel >= 0) & (rel < WINDOW) & (kpos >= 0)
    s = jnp.where(mask[None, :, None, None], s, -jnp.inf)
    sink = sinks.astype(jnp.float32).reshape(1, 1, SWA_KV_HEADS, SWA_GROUP, 1, 1)
    m = jnp.maximum(jnp.max(s, axis=-1, keepdims=True), sink)
    p = jnp.exp(s - m)
    p = p / (jnp.sum(p, axis=-1, keepdims=True) + jnp.exp(sink - m))
    o = jnp.einsum("bnhgqk,bnkhd->bnqhgd", p.astype(vb.dtype), vb)
    return o.reshape(B, S, SWA_WIDTH)


def sweep_query_blocks(block_fn, q):
    B, S = q.shape[:2]
    nb = S // BLOCK
    qb = jnp.moveaxis(q.reshape((B, nb, BLOCK) + q.shape[2:]), 1, 0)

    def body(args):
        qblk, n = args
        return block_fn(qblk, n * BLOCK + jnp.arange(BLOCK))

    o = lax.map(body, (qb, jnp.arange(nb)))
    o = jnp.moveaxis(o, 0, 1)
    return o.reshape((B, S) + o.shape[3:])


def causal_mla_attention(q, k, v):
    S = k.shape[1]
    kpos = jnp.arange(S)
    scale = (MLA_NOPE + MLA_ROPE) ** -0.5

    def block_fn(qblk, qpos):
        s = jnp.einsum("bqhd,bkhd->bhqk", qblk, k).astype(jnp.float32) * scale
        s = jnp.where(kpos[None, :] <= qpos[:, None], s, -jnp.inf)
        p = jax.nn.softmax(s, axis=-1)
        return jnp.einsum("bhqk,bkhd->bqhd", p.astype(v.dtype), v)

    return sweep_query_blocks(block_fn, q)


def causal_diff_attention(q, k, v, lam):
    S = k.shape[1]
    kpos = jnp.arange(S)
    scale = DIFF_QK ** -0.5

    def block_fn(qblk, qpos):
        s = jnp.einsum("bqhcd,bkhcd->bchqk", qblk, k).astype(jnp.float32) * scale
        s = jnp.where(kpos[None, :] <= qpos[:, None], s, -jnp.inf)
        p = jax.nn.softmax(s, axis=-1)
        a = p[:, 0] - lam * p[:, 1]
        return jnp.einsum("bhqk,bkhd->bqhd", a.astype(v.dtype), v)

    return sweep_query_blocks(block_fn, q)


def even_layer(h, w_in, conv_w, conv_b, gx_w, gx_b, ga_w, ga_b, lru_lambda, sinks, w_out):
    B, S, _ = h.shape
    z = h @ w_in
    lru_x, lru_gate, q, k, v, swa_gate = split_cols(z, EVEN_SPLITS)
    lru_x = causal_depthwise_conv(lru_x, conv_w, conv_b)
    y_a = rg_lru(lru_x, gx_w, gx_b, ga_w, ga_b, lru_lambda) * jax.nn.silu(lru_gate)
    q = q.reshape(B, S, SWA_Q_HEADS, SWA_HEAD_DIM)
    k = k.reshape(B, S, SWA_KV_HEADS, SWA_HEAD_DIM)
    v = v.reshape(B, S, SWA_KV_HEADS, SWA_HEAD_DIM)
    y_b = sliding_window_gqa_sinks(q, k, v, sinks) * jax.nn.silu(swa_gate)
    return jnp.concatenate([y_a, y_b], axis=-1) @ w_out


def odd_layer(h, w_in, q_norm, w_uq, kv_norm, w_ukv, lambda_q1, lambda_k1, lambda_q2, lambda_k2,
              subln, w_out):
    B, S, _ = h.shape
    pos = jnp.arange(S)
    z = h @ w_in
    c_q, c_kv, k_rope, mla_gate, dq, dk, dv, diff_gate = split_cols(z, ODD_SPLITS)
    q = (rmsnorm(c_q, q_norm) @ w_uq).reshape(B, S, MLA_HEADS, MLA_NOPE + MLA_ROPE)
    q = jnp.concatenate([q[..., :MLA_NOPE], rope(q[..., MLA_NOPE:], pos)], axis=-1)
    kv = (rmsnorm(c_kv, kv_norm) @ w_ukv).reshape(B, S, MLA_HEADS, MLA_NOPE + MLA_V)
    k_nope, v_c = kv[..., :MLA_NOPE], kv[..., MLA_NOPE:]
    k_r = rope(k_rope.reshape(B, S, 1, MLA_ROPE), pos)
    k_c = jnp.concatenate([k_nope, jnp.broadcast_to(k_r, (B, S, MLA_HEADS, MLA_ROPE))], axis=-1)
    y_c = causal_mla_attention(q, k_c, v_c).reshape(B, S, MLA_WIDTH) * jax.nn.silu(mla_gate)
    qd = dq.reshape(B, S, DIFF_HEADS, 2, DIFF_QK)
    kd = dk.reshape(B, S, DIFF_HEADS, 2, DIFF_QK)
    vd = dv.reshape(B, S, DIFF_HEADS, DIFF_V)
    lam = (jnp.exp(jnp.sum(lambda_q1.astype(jnp.float32) * lambda_k1.astype(jnp.float32)))
           - jnp.exp(jnp.sum(lambda_q2.astype(jnp.float32) * lambda_k2.astype(jnp.float32)))
           + DIFF_LAMBDA_INIT)
    od = causal_diff_attention(qd, kd, vd, lam)
    od = rmsnorm(od, subln) * (1.0 - DIFF_LAMBDA_INIT)
    y_d = od.reshape(B, S, DIFF_WIDTH) * jax.nn.silu(diff_gate)
    return jnp.concatenate([y_c, y_d], axis=-1) @ w_out


def setup_inputs(seed: int = 0) -> dict:
    key = jax.random.key(seed)
    ks = jax.random.split(key, 32)
    f32 = jnp.float32

    def dense(k, shape, fan_in):
        return jax.random.normal(k, shape, f32) * (fan_in ** -0.5)

    def gain(k, shape):
        return 1.0 + 0.05 * jax.random.normal(k, shape, f32)

    def small(k, shape, s=0.02):
        return s * jax.random.normal(k, shape, f32)

    u = jax.random.uniform(ks[10], (LRU_WIDTH,), f32, minval=0.9, maxval=0.999)
    a_base = u ** (1.0 / LRU_C)
    lru_lambda = jnp.log(a_base) - jnp.log1p(-a_base)

    return {
        "x": jax.random.normal(ks[0], (BATCH, SEQ, D_MODEL), f32),
        "norm_gains": gain(ks[1], (DEPTH, D_MODEL)),
        "final_norm_gain": gain(ks[2], (D_MODEL,)),
        "l0_w_in": dense(ks[3], (D_MODEL, EVEN_IN), D_MODEL),
        "l0_conv_w": dense(ks[4], (CONV_WIDTH, LRU_WIDTH), CONV_WIDTH),
        "l0_conv_b": small(ks[5], (LRU_WIDTH,)),
        "l0_gate_x_w": dense(ks[6], (LRU_BLOCKS, LRU_BLOCK_DIM, LRU_BLOCK_DIM), LRU_BLOCK_DIM),
        "l0_gate_x_b": small(ks[7], (LRU_BLOCKS, LRU_BLOCK_DIM)),
        "l0_gate_a_w": dense(ks[8], (LRU_BLOCKS, LRU_BLOCK_DIM, LRU_BLOCK_DIM), LRU_BLOCK_DIM),
        "l0_gate_a_b": small(ks[9], (LRU_BLOCKS, LRU_BLOCK_DIM)),
        "l0_lru_lambda": lru_lambda,
        "l0_sinks": 0.5 * jax.random.normal(ks[11], (SWA_Q_HEADS,), f32),
        "l0_w_out": dense(ks[12], (EVEN_MIX, D_MODEL), EVEN_MIX),
        "l1_w_in": dense(ks[13], (D_MODEL, ODD_IN), D_MODEL),
        "l1_q_norm": gain(ks[14], (MLA_Q_RANK,)),
        "l1_w_uq": dense(ks[15], (MLA_Q_RANK, MLA_HEADS * (MLA_NOPE + MLA_ROPE)), MLA_Q_RANK),
        "l1_kv_norm": gain(ks[16], (MLA_KV_RANK,)),
        "l1_w_ukv": dense(ks[17], (MLA_KV_RANK, MLA_HEADS * (MLA_NOPE + MLA_V)), MLA_KV_RANK),
        "l1_lambda_q1": small(ks[18], (DIFF_QK,), 0.1),
        "l1_lambda_k1": small(ks[19], (DIFF_QK,), 0.1),
        "l1_lambda_q2": small(ks[20], (DIFF_QK,), 0.1),
        "l1_lambda_k2": small(ks[21], (DIFF_QK,), 0.1),
        "l1_subln": gain(ks[22], (DIFF_V,)),
        "l1_w_out": dense(ks[23], (ODD_MIX, D_MODEL), ODD_MIX),
    }


def reference(x, norm_gains, final_norm_gain,
              l0_w_in, l0_conv_w, l0_conv_b, l0_gate_x_w, l0_gate_x_b, l0_gate_a_w, l0_gate_a_b,
              l0_lru_lambda, l0_sinks, l0_w_out,
              l1_w_in, l1_q_norm, l1_w_uq, l1_kv_norm, l1_w_ukv,
              l1_lambda_q1, l1_lambda_k1, l1_lambda_q2, l1_lambda_k2, l1_subln, l1_w_out):
    even_params = (l0_w_in, l0_conv_w, l0_conv_b, l0_gate_x_w, l0_gate_x_b, l0_gate_a_w,
                   l0_gate_a_b, l0_lru_lambda, l0_sinks, l0_w_out)
    odd_params = (l1_w_in, l1_q_norm, l1_w_uq, l1_kv_norm, l1_w_ukv, l1_lambda_q1, l1_lambda_k1,
                  l1_lambda_q2, l1_lambda_k2, l1_subln, l1_w_out)
    for layer in range(DEPTH):
        h = rmsnorm(x, norm_gains[layer])
        if layer % 2 == 0:
            x = x + even_layer(h, *even_params)
        else:
            x = x + odd_layer(h, *odd_params)
    return rmsnorm(x, final_norm_gain)
```

```python
import functools
import math

import jax
import jax.numpy as jnp
from jax import lax
from jax.experimental import pallas as pl
from jax.experimental.pallas import tpu as pltpu

F32 = jnp.float32
BF16 = jnp.bfloat16

D_MODEL = 2048
NORM_EPS = 1e-6
LANES = 128
SUBLANES = 8
VMEM_LIMIT = 56 * 1024 * 1024

LRU_WIDTH = 1024
LRU_BLOCKS = 8
LRU_BLOCK_DIM = LRU_WIDTH // LRU_BLOCKS
CONV_WIDTH = 4
LRU_C = 8.0

SWA_Q_HEADS = 16
SWA_KV_HEADS = 2
SWA_HEAD_DIM = 64
SWA_GROUP = SWA_Q_HEADS // SWA_KV_HEADS
SWA_WIDTH = SWA_Q_HEADS * SWA_HEAD_DIM
SWA_KV_WIDTH = SWA_KV_HEADS * SWA_HEAD_DIM
SWA_BLOCK = 128

MLA_HEADS = 8
MLA_Q_RANK = 768
MLA_KV_RANK = 512
MLA_NOPE = 128
MLA_ROPE = 64
MLA_V = 128
MLA_QK_PAD = 256
MLA_WIDTH = MLA_HEADS * MLA_V
ROPE_THETA = 10000.0

DIFF_HEADS = 8
DIFF_QK = 64
DIFF_V = 2 * DIFF_QK
DIFF_WIDTH = DIFF_HEADS * DIFF_V
DIFF_LAMBDA_INIT = 0.8 - 0.6 * math.exp(-0.3 * 1)

Z1_CQ = 0
Z1_KROPE = 768
Z1_CKV = 1024
Z1_MLA_GATE = 1536
Z1_DQ = 2560
Z1_DK = 3584
Z1_DV = 4608
Z1_DIFF_GATE = 5632
Z1_WIDTH = 6656

Z0_LRU_X = 0
Z0_LRU_GATE = 1024
Z0_Q = 2048
Z0_SWA_GATE = 3072
Z0_K = 4096
Z0_V = 4224
Z0_WIDTH = 4352


def _params(*sem):
    return pltpu.CompilerParams(dimension_semantics=sem, vmem_limit_bytes=VMEM_LIMIT)


def _silu(g):
    return g * jax.nn.sigmoid(g)


def _col_chunks(n, width):
    out, c = [], 0
    while c < n:
        w = min(width, n - c)
        out.append((c, w))
        c += w
    return out


def _norm_matmul_body(x_ref, g_ref, w_ref, o_ref, *, chunk):
    x = x_ref[...]
    ms = jnp.mean(x * x, axis=-1, keepdims=True)
    h = (x * lax.rsqrt(ms + NORM_EPS) * g_ref[...]).astype(BF16)
    for c0, cw in _col_chunks(w_ref.shape[1], chunk):
        o_ref[:, c0:c0 + cw] = jnp.dot(
            h, w_ref[:, c0:c0 + cw], preferred_element_type=F32).astype(o_ref.dtype)


def _norm_matmul(x, g, w, *, tm, n_blocks, chunk, name):
    m, d = x.shape
    n = w.shape[1]
    tn = n // n_blocks
    w_mode = dict(pipeline_mode=pl.Buffered(1)) if n_blocks == 1 else {}
    return pl.pallas_call(
        functools.partial(_norm_matmul_body, chunk=chunk),
        grid=(n_blocks, m // tm),
        in_specs=[
            pl.BlockSpec((tm, d), lambda j, i: (i, 0)),
            pl.BlockSpec((1, d), lambda j, i: (0, 0)),
            pl.BlockSpec((d, tn), lambda j, i: (0, j), **w_mode),
        ],
        out_specs=pl.BlockSpec((tm, tn), lambda j, i: (i, j)),
        out_shape=jax.ShapeDtypeStruct((m, n), BF16),
        compiler_params=_params("parallel", "parallel"),
        name=name,
    )(x, g.reshape(1, d), w)


def _out_proj_body(ya_ref, yb_ref, w_ref, x_ref, g_ref, o_ref, *, chunk, final_norm):
    ya = ya_ref[...]
    yb = yb_ref[...]
    ka = ya.shape[1]
    ssq = None
    for c0, cw in _col_chunks(w_ref.shape[1], chunk):
        r = jnp.dot(ya, w_ref[:ka, c0:c0 + cw], preferred_element_type=F32)
        r = r + jnp.dot(yb, w_ref[ka:, c0:c0 + cw], preferred_element_type=F32)
        r = r + x_ref[:, c0:c0 + cw]
        o_ref[:, c0:c0 + cw] = r
        if final_norm:
            part = jnp.sum(r * r, axis=-1, keepdims=True)
            ssq = part if ssq is None else ssq + part
    if final_norm:
        inv = lax.rsqrt(ssq / w_ref.shape[1] + NORM_EPS)
        o_ref[...] = o_ref[...] * inv * g_ref[...]


def _out_proj(ya, yb, w, x, g, *, tm, chunk, final_norm, name):
    m, d = x.shape
    ka, kb = ya.shape[1], yb.shape[1]
    return pl.pallas_call(
        functools.partial(_out_proj_body, chunk=chunk, final_norm=final_norm),
        grid=(m // tm,),
        in_specs=[
            pl.BlockSpec((tm, ka), lambda i: (i, 0)),
            pl.BlockSpec((tm, kb), lambda i: (i, 0)),
            pl.BlockSpec((ka + kb, d), lambda i: (0, 0)),
            pl.BlockSpec((tm, d), lambda i: (i, 0)),
            pl.BlockSpec((1, d), lambda i: (0, 0)),
        ],
        out_specs=pl.BlockSpec((tm, d), lambda i: (i, 0)),
        out_shape=jax.ShapeDtypeStruct((m, d), F32),
        compiler_params=_params("parallel"),
        name=name,
    )(ya, yb, w, x, g.reshape(1, d))


def _lru_body(x_ref, gate_ref, cw_ref, cb_ref, gxw_ref, gxb_ref, gaw_ref, gab_ref, lam_ref,
              o_ref, xbuf, hcar, a_scr, b_scr, *, t):
    c = pl.program_id(1)

    @pl.when(c == 0)
    def _():
        xbuf[0:SUBLANES, :] = jnp.zeros((SUBLANES, LRU_WIDTH), F32)
        hcar[...] = jnp.zeros((SUBLANES, LRU_WIDTH), F32)

    xbuf[SUBLANES:SUBLANES + t, :] = x_ref[...].astype(F32)
    xc = cb_ref[...] + cw_ref[0:1, :] * xbuf[SUBLANES - 3:SUBLANES - 3 + t, :]
    for k in range(1, CONV_WIDTH):
        r0 = SUBLANES - 3 + k
        xc = xc + cw_ref[k:k + 1, :] * xbuf[r0:r0 + t, :]
    xbuf[0:SUBLANES, :] = xbuf[t:t + SUBLANES, :]

    xcb = xc.astype(BF16)
    gi_parts, gr_parts = [], []
    for n in range(LRU_BLOCKS):
        xb = xcb[:, n * LRU_BLOCK_DIM:(n + 1) * LRU_BLOCK_DIM]
        gi_parts.append(jnp.dot(xb, gxw_ref[n], preferred_element_type=F32))
        gr_parts.append(jnp.dot(xb, gaw_ref[n], preferred_element_type=F32))
    gate_i = jax.nn.sigmoid(jnp.concatenate(gi_parts, axis=-1) + gxb_ref[...])
    gate_r = jax.nn.sigmoid(jnp.concatenate(gr_parts, axis=-1) + gab_ref[...])

    neg_lam = -lam_ref[...]
    softplus = jnp.maximum(neg_lam, 0.0) + jnp.log1p(jnp.exp(-jnp.abs(neg_lam)))
    log_a = (-LRU_C * softplus) * gate_r
    a = jnp.exp(log_a)
    mult = jnp.sqrt(1.0 - a * a)
    u = mult * gate_i * xc

    row = lax.broadcasted_iota(jnp.int32, (t, LRU_WIDTH), 0) % SUBLANES
    d = 1
    while d < SUBLANES:
        a_sh = pltpu.roll(a, d, axis=0)
        u_sh = pltpu.roll(u, d, axis=0)
        ok = row >= d
        u = jnp.where(ok, a * u_sh + u, u)
        a = jnp.where(ok, a * a_sh, a)
        d *= 2
    a_scr[...] = a
    b_scr[...] = u

    def group(gidx, h_prev):
        r0 = pl.multiple_of(gidx * SUBLANES, SUBLANES)
        hg = a_scr[pl.ds(r0, SUBLANES), :] * h_prev + b_scr[pl.ds(r0, SUBLANES), :]
        b_scr[pl.ds(r0, SUBLANES), :] = hg
        return jnp.broadcast_to(hg[SUBLANES - 1:SUBLANES, :], (SUBLANES, LRU_WIDTH))

    hcar[...] = lax.fori_loop(0, t // SUBLANES, group, hcar[...])

    g = gate_ref[...].astype(F32)
    o_ref[...] = (b_scr[...] * _silu(g)).astype(o_ref.dtype)


def _lru(z0, conv_w, conv_b, gx_w, gx_b, ga_w, ga_b, lam, *, batch, seq, t):
    m = batch * seq
    nc = seq // t
    row_spec = lambda col: pl.BlockSpec((t, LRU_WIDTH), lambda b, c: (b * nc + c, col))
    full = lambda shape: pl.BlockSpec(shape, lambda b, c: (0,) * len(shape))
    return pl.pallas_call(
        functools.partial(_lru_body, t=t),
        grid=(batch, nc),
        in_specs=[
            row_spec(Z0_LRU_X // LRU_WIDTH),
            row_spec(Z0_LRU_GATE // LRU_WIDTH),
            full((CONV_WIDTH, LRU_WIDTH)),
            full((1, LRU_WIDTH)),
            full((LRU_BLOCKS, LRU_BLOCK_DIM, LRU_BLOCK_DIM)),
            full((1, LRU_WIDTH)),
            full((LRU_BLOCKS, LRU_BLOCK_DIM, LRU_BLOCK_DIM)),
            full((1, LRU_WIDTH)),
            full((1, LRU_WIDTH)),
        ],
        out_specs=pl.BlockSpec((t, LRU_WIDTH), lambda b, c: (b * nc + c, 0)),
        out_shape=jax.ShapeDtypeStruct((m, LRU_WIDTH), BF16),
        scratch_shapes=[
            pltpu.VMEM((t + SUBLANES, LRU_WIDTH), F32),
            pltpu.VMEM((SUBLANES, LRU_WIDTH), F32),
            pltpu.VMEM((t, LRU_WIDTH), F32),
            pltpu.VMEM((t, LRU_WIDTH), F32),
        ],
        compiler_params=_params("parallel", "arbitrary"),
        name="rg_lru",
    )(z0, z0, conv_w, conv_b.reshape(1, -1), gx_w.astype(BF16), gx_b.reshape(1, -1),
      ga_w.astype(BF16), ga_b.reshape(1, -1), lam.reshape(1, -1))


def _swa_body(sink_ref, q_ref, kp_ref, kc_ref, vp_ref, vc_ref, gate_ref, o_ref):
    n = pl.program_id(1)
    blk = SWA_BLOCK
    qi = lax.broadcasted_iota(jnp.int32, (blk, 2 * blk), 0)
    ki = lax.broadcasted_iota(jnp.int32, (blk, 2 * blk), 1)
    rel = qi + blk - ki
    mask = (rel >= 0) & (rel < blk) & ((ki >= blk) | (n > 0))
    k2 = jnp.concatenate([kp_ref[...], kc_ref[...]], axis=0)
    v2 = jnp.concatenate([vp_ref[...], vc_ref[...]], axis=0)
    scale = SWA_HEAD_DIM ** -0.5
    for g in range(SWA_KV_HEADS):
        kg = k2[:, g * SWA_HEAD_DIM:(g + 1) * SWA_HEAD_DIM]
        vg = v2[:, g * SWA_HEAD_DIM:(g + 1) * SWA_HEAD_DIM]
        for hh in range(SWA_GROUP):
            h = g * SWA_GROUP + hh
            c0 = h * SWA_HEAD_DIM
            qh = q_ref[:, c0:c0 + SWA_HEAD_DIM]
            s = lax.dot_general(qh, kg, (((1,), (1,)), ((), ())),
                                preferred_element_type=F32) * scale
            s = jnp.where(mask, s, -jnp.inf)
            sink = sink_ref[h]
            mx = jnp.maximum(jnp.max(s, axis=-1, keepdims=True), sink)
            p = jnp.exp(s - mx)
            den = jnp.sum(p, axis=-1, keepdims=True) + jnp.exp(sink - mx)
            p = p / den
            o = jnp.dot(p.astype(BF16), vg, preferred_element_type=F32)
            gt = gate_ref[:, c0:c0 + SWA_HEAD_DIM].astype(F32)
            o_ref[:, c0:c0 + SWA_HEAD_DIM] = (o * _silu(gt)).astype(o_ref.dtype)


def _swa(z0, sinks, *, batch, seq):
    m = batch * seq
    nb = seq // SWA_BLOCK
    wide = lambda col: pl.BlockSpec((SWA_BLOCK, SWA_WIDTH), lambda b, n: (b * nb + n, col))
    cur = lambda col: pl.BlockSpec((SWA_BLOCK, SWA_KV_WIDTH), lambda b, n: (b * nb + n, col))
    prev = lambda col: pl.BlockSpec(
        (SWA_BLOCK, SWA_KV_WIDTH), lambda b, n: (b * nb + jnp.maximum(n - 1, 0), col))
    return pl.pallas_call(
        _swa_body,
        grid=(batch, nb),
        in_specs=[
            pl.BlockSpec(memory_space=pltpu.SMEM),
            wide(Z0_Q // SWA_WIDTH),
            prev(Z0_K // SWA_KV_WIDTH), cur(Z0_K // SWA_KV_WIDTH),
            prev(Z0_V // SWA_KV_WIDTH), cur(Z0_V // SWA_KV_WIDTH),
            wide(Z0_SWA_GATE // SWA_WIDTH),
        ],
        out_specs=pl.BlockSpec((SWA_BLOCK, SWA_WIDTH), lambda b, n: (b * nb + n, 0)),
        out_shape=jax.ShapeDtypeStruct((m, SWA_WIDTH), BF16),
        compiler_params=_params("parallel", "parallel"),
        name="swa",
    )(sinks, z0, z0, z0, z0, z0, z0)


def _rope128(x, cos2, sin2):
    lane = lax.broadcasted_iota(jnp.int32, x.shape, 1)
    swapped = jnp.where(lane < MLA_ROPE // 2,
                        pltpu.roll(x, LANES - MLA_ROPE // 2, axis=1),
                        pltpu.roll(x, MLA_ROPE // 2, axis=1))
    return x * cos2 + swapped * sin2


def _mla_up_body(cq_ref, ckv_ref, kr_ref, qn_ref, kvn_ref, wq_ref, wkv_ref, cos_ref, sin_ref,
                 q_ref, k_ref, v_ref):
    cos2 = cos_ref[...]
    sin2 = sin_ref[...]

    def normed(x, g):
        ms = jnp.mean(x * x, axis=-1, keepdims=True)
        return (x * lax.rsqrt(ms + NORM_EPS) * g).astype(BF16)

    hq = normed(cq_ref[:, :MLA_Q_RANK].astype(F32), qn_ref[...])
    hkv = normed(ckv_ref[...].astype(F32), kvn_ref[...])
    lane = lax.broadcasted_iota(jnp.int32, kr_ref.shape, 1)
    kr = jnp.where(lane < MLA_ROPE, kr_ref[...].astype(F32), 0.0)
    kr = _rope128(kr, cos2, sin2).astype(BF16)
    scale = (MLA_NOPE + MLA_ROPE) ** -0.5
    for h in range(MLA_HEADS):
        c0 = h * MLA_QK_PAD
        qh = jnp.dot(hq, wq_ref[:, c0:c0 + MLA_QK_PAD], preferred_element_type=F32)
        q_ref[:, c0:c0 + MLA_NOPE] = (qh[:, :MLA_NOPE] * scale).astype(BF16)
        q_ref[:, c0 + MLA_NOPE:c0 + MLA_QK_PAD] = (
            _rope128(qh[:, MLA_NOPE:], cos2, sin2) * scale).astype(BF16)
        kvh = jnp.dot(hkv, wkv_ref[:, c0:c0 + MLA_NOPE + MLA_V], preferred_element_type=F32)
        k_ref[:, c0:c0 + MLA_NOPE] = kvh[:, :MLA_NOPE].astype(BF16)
        k_ref[:, c0 + MLA_NOPE:c0 + MLA_QK_PAD] = kr
        v_ref[:, h * MLA_V:(h + 1) * MLA_V] = kvh[:, MLA_NOPE:].astype(BF16)


def _mla_up(z1, q_norm, kv_norm, wq_pad, wkv, cos2, sin2, *, seq, tm):
    m = z1.shape[0]
    npos = seq // tm
    full = lambda shape: pl.BlockSpec(shape, lambda i: (0,) * len(shape))
    return pl.pallas_call(
        _mla_up_body,
        grid=(m // tm,),
        in_specs=[
            pl.BlockSpec((tm, 1024), lambda i: (i, Z1_CQ // 1024)),
            pl.BlockSpec((tm, MLA_KV_RANK), lambda i: (i, Z1_CKV // MLA_KV_RANK)),
            pl.BlockSpec((tm, LANES), lambda i: (i, Z1_KROPE // LANES)),
            full((1, MLA_Q_RANK)),
            full((1, MLA_KV_RANK)),
            full((MLA_Q_RANK, MLA_HEADS * MLA_QK_PAD)),
            full((MLA_KV_RANK, MLA_HEADS * (MLA_NOPE + MLA_V))),
            pl.BlockSpec((tm, LANES), lambda i: (i % npos, 0)),
            pl.BlockSpec((tm, LANES), lambda i: (i % npos, 0)),
        ],
        out_specs=[
            pl.BlockSpec((tm, MLA_HEADS * MLA_QK_PAD), lambda i: (i, 0)),
            pl.BlockSpec((tm, MLA_HEADS * MLA_QK_PAD), lambda i: (i, 0)),
            pl.BlockSpec((tm, MLA_WIDTH), lambda i: (i, 0)),
        ],
        out_shape=[
            jax.ShapeDtypeStruct((m, MLA_HEADS * MLA_QK_PAD), BF16),
            jax.ShapeDtypeStruct((m, MLA_HEADS * MLA_QK_PAD), BF16),
            jax.ShapeDtypeStruct((m, MLA_WIDTH), BF16),
        ],
        compiler_params=_params("parallel"),
        name="mla_up",
    )(z1, z1, z1, q_norm.reshape(1, -1), kv_norm.reshape(1, -1), wq_pad, wkv, cos2, sin2)


def _online_softmax_step(s, v, carry):
    m_prev, l_prev, acc = carry
    m_new = jnp.maximum(m_prev, jnp.max(s, axis=-1, keepdims=True))
    p = jnp.exp(s - m_new)
    alpha = jnp.exp(m_prev - m_new)
    l_new = alpha * l_prev + jnp.sum(p, axis=-1, keepdims=True)
    acc = alpha * acc + jnp.dot(p.astype(BF16), v, preferred_element_type=F32)
    return m_new, l_new, acc


def _softmax_init(bq, dv):
    return (jnp.full((bq, 1), -jnp.inf, F32), jnp.zeros((bq, 1), F32), jnp.zeros((bq, dv), F32))


def _qk(q, k):
    return lax.dot_general(q, k, (((1,), (1,)), ((), ())), preferred_element_type=F32)


def _causal_mask(bq):
    row = lax.broadcasted_iota(jnp.int32, (bq, bq), 0)
    col = lax.broadcasted_iota(jnp.int32, (bq, bq), 1)
    return col <= row


def _mla_attn_body(q_ref, k_ref, v_ref, gate_ref, o_ref, *, bq):
    i = pl.program_id(2)
    q = q_ref[...]

    def body(j, carry):
        r0 = pl.multiple_of(j * bq, bq)
        s = _qk(q, k_ref[pl.ds(r0, bq), :])
        return _online_softmax_step(s, v_ref[pl.ds(r0, bq), :], carry)

    carry = lax.fori_loop(0, i, body, _softmax_init(bq, MLA_V))
    r0 = pl.multiple_of(i * bq, bq)
    s = jnp.where(_causal_mask(bq), _qk(q, k_ref[pl.ds(r0, bq), :]), -jnp.inf)
    _, l, acc = _online_softmax_step(s, v_ref[pl.ds(r0, bq), :], carry)
    g = gate_ref[...].astype(F32)
    o_ref[...] = (acc / l * _silu(g)).astype(o_ref.dtype)


def _mla_attn(q, k, v, z1, *, batch, seq, bq):
    m = batch * seq
    nq = seq // bq
    return pl.pallas_call(
        functools.partial(_mla_attn_body, bq=bq),
        grid=(batch, MLA_HEADS, nq),
        in_specs=[
            pl.BlockSpec((bq, MLA_QK_PAD), lambda b, h, i: (b * nq + i, h)),
            pl.BlockSpec((seq, MLA_QK_PAD), lambda b, h, i: (b, h)),
            pl.BlockSpec((seq, MLA_V), lambda b, h, i: (b, h)),
            pl.BlockSpec((bq, MLA_V), lambda b, h, i: (b * nq + i, Z1_MLA_GATE // MLA_V + h)),
        ],
        out_specs=pl.BlockSpec((bq, MLA_V), lambda b, h, i: (b * nq + i, h)),
        out_shape=jax.ShapeDtypeStruct((m, MLA_WIDTH), BF16),
        compiler_params=_params("parallel", "parallel", "arbitrary"),
        name="mla_attn",
    )(q, k, v, z1)


def _diff_attn_body(q_ref, k_ref, v_ref, gate_ref, lq1_ref, lk1_ref, lq2_ref, lk2_ref, sub_ref,
                    o_ref, *, bq):
    i = pl.program_id(2)
    q = q_ref[...]
    lane = lax.broadcasted_iota(jnp.int32, q.shape, 1)
    zero = jnp.zeros_like(q)
    q1 = jnp.where(lane < DIFF_QK, q, zero)
    q2 = jnp.where(lane >= DIFF_QK, q, zero)
    scale = DIFF_QK ** -0.5

    def step(r0, carry, mask):
        c1, c2 = carry
        k = k_ref[pl.ds(r0, bq), :]
        v = v_ref[pl.ds(r0, bq), :]
        s1 = _qk(q1, k) * scale
        s2 = _qk(q2, k) * scale
        if mask is not None:
            s1 = jnp.where(mask, s1, -jnp.inf)
            s2 = jnp.where(mask, s2, -jnp.inf)
        return _online_softmax_step(s1, v, c1), _online_softmax_step(s2, v, c2)

    def body(j, carry):
        return step(pl.multiple_of(j * bq, bq), carry, None)

    init = (_softmax_init(bq, DIFF_V), _softmax_init(bq, DIFF_V))
    carry = lax.fori_loop(0, i, body, init)
    (_, l1, a1), (_, l2, a2) = step(pl.multiple_of(i * bq, bq), carry, _causal_mask(bq))

    lam = (jnp.exp(jnp.sum(lq1_ref[...] * lk1_ref[...], axis=-1, keepdims=True))
           - jnp.exp(jnp.sum(lq2_ref[...] * lk2_ref[...], axis=-1, keepdims=True))
           + DIFF_LAMBDA_INIT)
    od = a1 / l1 - lam * (a2 / l2)
    ms = jnp.mean(od * od, axis=-1, keepdims=True)
    od = od * lax.rsqrt(ms + NORM_EPS) * sub_ref[...] * (1.0 - DIFF_LAMBDA_INIT)
    g = gate_ref[...].astype(F32)
    o_ref[...] = (od * _silu(g)).astype(o_ref.dtype)


def _diff_attn(z1, lq1, lk1, lq2, lk2, subln, *, batch, seq, bq):
    m = batch * seq
    nq = seq // bq
    vec = lambda n: pl.BlockSpec((1, n), lambda b, h, i: (0, 0))
    return pl.pallas_call(
        functools.partial(_diff_attn_body, bq=bq),
        grid=(batch, DIFF_HEADS, nq),
        in_specs=[
            pl.BlockSpec((bq, DIFF_V), lambda b, h, i: (b * nq + i, Z1_DQ // DIFF_V + h)),
            pl.BlockSpec((seq, DIFF_V), lambda b, h, i: (b, Z1_DK // DIFF_V + h)),
            pl.BlockSpec((seq, DIFF_V), lambda b, h, i: (b, Z1_DV // DIFF_V + h)),
            pl.BlockSpec((bq, DIFF_V), lambda b, h, i: (b * nq + i, Z1_DIFF_GATE // DIFF_V + h)),
            vec(DIFF_QK), vec(DIFF_QK), vec(DIFF_QK), vec(DIFF_QK), vec(DIFF_V),
        ],
        out_specs=pl.BlockSpec((bq, DIFF_V), lambda b, h, i: (b * nq + i, h)),
        out_shape=jax.ShapeDtypeStruct((m, DIFF_WIDTH), BF16),
        compiler_params=_params("parallel", "parallel", "arbitrary"),
        name="diff_attn",
    )(z1, z1, z1, z1, lq1.reshape(1, -1), lk1.reshape(1, -1), lq2.reshape(1, -1),
      lk2.reshape(1, -1), subln.reshape(1, -1))


def _layer0_w_in(w):
    q0 = 2 * LRU_WIDTH
    k0 = q0 + SWA_WIDTH
    g0 = k0 + 2 * SWA_KV_WIDTH
    return jnp.concatenate([w[:, :k0], w[:, g0:], w[:, k0:g0]], axis=1).astype(BF16)


def _layer1_w_in(w):
    ckv0 = MLA_Q_RANK
    kr0 = ckv0 + MLA_KV_RANK
    rest0 = kr0 + MLA_ROPE
    pad = jnp.zeros((w.shape[0], Z1_CKV - Z1_KROPE - MLA_ROPE), w.dtype)
    return jnp.concatenate(
        [w[:, :ckv0], w[:, kr0:rest0], pad, w[:, ckv0:kr0], w[:, rest0:]], axis=1).astype(BF16)


def _pad_w_uq(w):
    w = w.reshape(MLA_Q_RANK, MLA_HEADS, MLA_NOPE + MLA_ROPE)
    w = jnp.pad(w, ((0, 0), (0, 0), (0, MLA_QK_PAD - MLA_NOPE - MLA_ROPE)))
    return w.reshape(MLA_Q_RANK, MLA_HEADS * MLA_QK_PAD).astype(BF16)


def _rope_tables(seq):
    half = MLA_ROPE // 2
    freq = ROPE_THETA ** (-jnp.arange(half, dtype=F32) / half)
    ang = jnp.arange(seq).astype(F32)[:, None] * freq[None, :]
    cos, sin = jnp.cos(ang), jnp.sin(ang)
    zeros = jnp.zeros((seq, LANES - MLA_ROPE), F32)
    return (jnp.concatenate([cos, cos, zeros], axis=1),
            jnp.concatenate([-sin, sin, zeros], axis=1))


def kernel(x, norm_gains, final_norm_gain, l0_w_in, l0_conv_w, l0_conv_b, l0_gate_x_w, l0_gate_x_b, l0_gate_a_w, l0_gate_a_b, l0_lru_lambda, l0_sinks, l0_w_out, l1_w_in, l1_q_norm, l1_w_uq, l1_kv_norm, l1_w_ukv, l1_lambda_q1, l1_lambda_k1, l1_lambda_q2, l1_lambda_k2, l1_subln, l1_w_out):
    batch, seq, d = x.shape
    m = batch * seq
    x0 = x.reshape(m, d)
    tm = min(512, seq)
    bq = min(512, seq)
    t_lru = min(256, seq)

    z0 = _norm_matmul(x0, norm_gains[0], _layer0_w_in(l0_w_in),
                      tm=tm, n_blocks=1, chunk=512, name="in_proj0")
    y_a = _lru(z0, l0_conv_w, l0_conv_b, l0_gate_x_w, l0_gate_x_b, l0_gate_a_w, l0_gate_a_b,
               l0_lru_lambda, batch=batch, seq=seq, t=t_lru)
    y_b = _swa(z0, l0_sinks, batch=batch, seq=seq)
    x1 = _out_proj(y_a, y_b, l0_w_out.astype(BF16), x0, final_norm_gain,
                   tm=tm, chunk=512, final_norm=False, name="out_proj0")

    z1 = _norm_matmul(x1, norm_gains[1], _layer1_w_in(l1_w_in),
                      tm=tm, n_blocks=2, chunk=512, name="in_proj1")
    cos2, sin2 = _rope_tables(seq)
    q, k, v = _mla_up(z1, l1_q_norm, l1_kv_norm, _pad_w_uq(l1_w_uq), l1_w_ukv.astype(BF16),
                      cos2, sin2, seq=seq, tm=tm)
    y_c = _mla_attn(q, k, v, z1, batch=batch, seq=seq, bq=bq)
    y_d = _diff_attn(z1, l1_lambda_q1, l1_lambda_k1, l1_lambda_q2, l1_lambda_k2, l1_subln,
                     batch=batch, seq=seq, bq=bq)
    out = _out_proj(y_c, y_d, l1_w_out.astype(BF16), x1, final_norm_gain,
                    tm=tm, chunk=512, final_norm=True, name="out_proj1")
    return out.reshape(batch, seq, d)
```

```python
import functools
import math

import jax
import jax.numpy as jnp
from jax import lax
from jax.experimental import pallas as pl
from jax.experimental.pallas import tpu as pltpu

F32 = jnp.float32
BF16 = jnp.bfloat16

D_MODEL = 2048
NORM_EPS = 1e-6
LANES = 128
SUBLANES = 8
VMEM_LIMIT = 56 * 1024 * 1024

LRU_WIDTH = 1024
LRU_BLOCKS = 8
LRU_BLOCK_DIM = LRU_WIDTH // LRU_BLOCKS
CONV_WIDTH = 4
LRU_C = 8.0

SWA_Q_HEADS = 16
SWA_KV_HEADS = 2
SWA_HEAD_DIM = 64
SWA_GROUP = SWA_Q_HEADS // SWA_KV_HEADS
SWA_WIDTH = SWA_Q_HEADS * SWA_HEAD_DIM
SWA_KV_WIDTH = SWA_KV_HEADS * SWA_HEAD_DIM
SWA_BLOCK = 128

MLA_HEADS = 8
MLA_Q_RANK = 768
MLA_KV_RANK = 512
MLA_NOPE = 128
MLA_ROPE = 64
MLA_V = 128
MLA_QK_PAD = 256
MLA_WIDTH = MLA_HEADS * MLA_V
ROPE_THETA = 10000.0

DIFF_HEADS = 8
DIFF_QK = 64
DIFF_V = 2 * DIFF_QK
DIFF_WIDTH = DIFF_HEADS * DIFF_V
DIFF_LAMBDA_INIT = 0.8 - 0.6 * math.exp(-0.3 * 1)

Z1_CQ = 0
Z1_KROPE = 768
Z1_CKV = 1024
Z1_MLA_GATE = 1536
Z1_DQ = 2560
Z1_DK = 3584
Z1_DV = 4608
Z1_DIFF_GATE = 5632
Z1_WIDTH = 6656

Z0_LRU_X = 0
Z0_LRU_GATE = 1024
Z0_Q = 2048
Z0_SWA_GATE = 3072
Z0_K = 4096
Z0_V = 4224
Z0_WIDTH = 4352


def _params(*sem):
    return pltpu.CompilerParams(dimension_semantics=sem, vmem_limit_bytes=VMEM_LIMIT)


def _silu(g):
    return g * jax.nn.sigmoid(g)


def _col_chunks(n, width):
    out, c = [], 0
    while c < n:
        w = min(width, n - c)
        out.append((c, w))
        c += w
    return out


def _norm_matmul_body(x_ref, g_ref, w_ref, o_ref, *, chunk):
    x = x_ref[...]
    ms = jnp.mean(x * x, axis=-1, keepdims=True)
    h = (x * lax.rsqrt(ms + NORM_EPS) * g_ref[...]).astype(BF16)
    for c0, cw in _col_chunks(w_ref.shape[1], chunk):
        o_ref[:, c0:c0 + cw] = jnp.dot(
            h, w_ref[:, c0:c0 + cw], preferred_element_type=F32).astype(o_ref.dtype)


def _norm_matmul(x, g, w, *, tm, n_blocks, chunk, name):
    m, d = x.shape
    n = w.shape[1]
    tn = n // n_blocks
    w_mode = dict(pipeline_mode=pl.Buffered(1)) if n_blocks == 1 else {}
    return pl.pallas_call(
        functools.partial(_norm_matmul_body, chunk=chunk),
        grid=(n_blocks, m // tm),
        in_specs=[
            pl.BlockSpec((tm, d), lambda j, i: (i, 0)),
            pl.BlockSpec((1, d), lambda j, i: (0, 0)),
            pl.BlockSpec((d, tn), lambda j, i: (0, j), **w_mode),
        ],
        out_specs=pl.BlockSpec((tm, tn), lambda j, i: (i, j)),
        out_shape=jax.ShapeDtypeStruct((m, n), BF16),
        compiler_params=_params("parallel", "parallel"),
        name=name,
    )(x, g.reshape(1, d), w)


def _out_proj_body(ya_ref, yb_ref, w_ref, x_ref, g_ref, o_ref, *, chunk, final_norm):
    ya = ya_ref[...]
    yb = yb_ref[...]
    ka = ya.shape[1]
    ssq = None
    for c0, cw in _col_chunks(w_ref.shape[1], chunk):
        r = jnp.dot(ya, w_ref[:ka, c0:c0 + cw], preferred_element_type=F32)
        r = r + jnp.dot(yb, w_ref[ka:, c0:c0 + cw], preferred_element_type=F32)
        r = r + x_ref[:, c0:c0 + cw]
        o_ref[:, c0:c0 + cw] = r
        if final_norm:
            part = jnp.sum(r * r, axis=-1, keepdims=True)
            ssq = part if ssq is None else ssq + part
    if final_norm:
        inv = lax.rsqrt(ssq / w_ref.shape[1] + NORM_EPS)
        o_ref[...] = o_ref[...] * inv * g_ref[...]


def _out_proj(ya, yb, w, x, g, *, tm, chunk, final_norm, name):
    m, d = x.shape
    ka, kb = ya.shape[1], yb.shape[1]
    return pl.pallas_call(
        functools.partial(_out_proj_body, chunk=chunk, final_norm=final_norm),
        grid=(m // tm,),
        in_specs=[
            pl.BlockSpec((tm, ka), lambda i: (i, 0)),
            pl.BlockSpec((tm, kb), lambda i: (i, 0)),
            pl.BlockSpec((ka + kb, d), lambda i: (0, 0)),
            pl.BlockSpec((tm, d), lambda i: (i, 0)),
            pl.BlockSpec((1, d), lambda i: (0, 0)),
        ],
        out_specs=pl.BlockSpec((tm, d), lambda i: (i, 0)),
        out_shape=jax.ShapeDtypeStruct((m, d), F32),
        compiler_params=_params("parallel"),
        name=name,
    )(ya, yb, w, x, g.reshape(1, d))


def _lru_body(x_ref, gate_ref, cw_ref, cb_ref, gxw_ref, gxb_ref, gaw_ref, gab_ref, lam_ref,
              o_ref, xbuf, hcar, a_scr, b_scr, *, t):
    c = pl.program_id(1)

    @pl.when(c == 0)
    def _():
        xbuf[0:SUBLANES, :] = jnp.zeros((SUBLANES, LRU_WIDTH), F32)
        hcar[...] = jnp.zeros((SUBLANES, LRU_WIDTH), F32)

    xbuf[SUBLANES:SUBLANES + t, :] = x_ref[...].astype(F32)
    xc = cb_ref[...] + cw_ref[0:1, :] * xbuf[SUBLANES - 3:SUBLANES - 3 + t, :]
    for k in range(1, CONV_WIDTH):
        r0 = SUBLANES - 3 + k
        xc = xc + cw_ref[k:k + 1, :] * xbuf[r0:r0 + t, :]
    xbuf[0:SUBLANES, :] = xbuf[t:t + SUBLANES, :]

    xcb = xc.astype(BF16)
    gi_parts, gr_parts = [], []
    for n in range(LRU_BLOCKS):
        xb = xcb[:, n * LRU_BLOCK_DIM:(n + 1) * LRU_BLOCK_DIM]
        gi_parts.append(jnp.dot(xb, gxw_ref[n], preferred_element_type=F32))
        gr_parts.append(jnp.dot(xb, gaw_ref[n], preferred_element_type=F32))
    gate_i = jax.nn.sigmoid(jnp.concatenate(gi_parts, axis=-1) + gxb_ref[...])
    gate_r = jax.nn.sigmoid(jnp.concatenate(gr_parts, axis=-1) + gab_ref[...])

    neg_lam = -lam_ref[...]
    softplus = jnp.maximum(neg_lam, 0.0) + jnp.log1p(jnp.exp(-jnp.abs(neg_lam)))
    log_a = (-LRU_C * softplus) * gate_r
    a = jnp.exp(log_a)
    mult = jnp.sqrt(1.0 - a * a)
    u = mult * gate_i * xc

    row = lax.broadcasted_iota(jnp.int32, (t, LRU_WIDTH), 0) % SUBLANES
    d = 1
    while d < SUBLANES:
        a_sh = pltpu.roll(a, d, axis=0)
        u_sh = pltpu.roll(u, d, axis=0)
        ok = row >= d
        u = jnp.where(ok, a * u_sh + u, u)
        a = jnp.where(ok, a * a_sh, a)
        d *= 2
    a_scr[...] = a
    b_scr[...] = u

    def group(gidx, h_prev):
        r0 = pl.multiple_of(gidx * SUBLANES, SUBLANES)
        hg = a_scr[pl.ds(r0, SUBLANES), :] * h_prev + b_scr[pl.ds(r0, SUBLANES), :]
        b_scr[pl.ds(r0, SUBLANES), :] = hg
        return jnp.broadcast_to(hg[SUBLANES - 1:SUBLANES, :], (SUBLANES, LRU_WIDTH))

    hcar[...] = lax.fori_loop(0, t // SUBLANES, group, hcar[...])

    g = gate_ref[...].astype(F32)
    o_ref[...] = (b_scr[...] * _silu(g)).astype(o_ref.dtype)


def _lru(z0, conv_w, conv_b, gx_w, gx_b, ga_w, ga_b, lam, *, batch, seq, t):
    m = batch * seq
    nc = seq // t
    row_spec = lambda col: pl.BlockSpec((t, LRU_WIDTH), lambda b, c: (b * nc + c, col))
    full = lambda shape: pl.BlockSpec(shape, lambda b, c: (0,) * len(shape))
    return pl.pallas_call(
        functools.partial(_lru_body, t=t),
        grid=(batch, nc),
        in_specs=[
            row_spec(Z0_LRU_X // LRU_WIDTH),
            row_spec(Z0_LRU_GATE // LRU_WIDTH),
            full((CONV_WIDTH, LRU_WIDTH)),
            full((1, LRU_WIDTH)),
            full((LRU_BLOCKS, LRU_BLOCK_DIM, LRU_BLOCK_DIM)),
            full((1, LRU_WIDTH)),
            full((LRU_BLOCKS, LRU_BLOCK_DIM, LRU_BLOCK_DIM)),
            full((1, LRU_WIDTH)),
            full((1, LRU_WIDTH)),
        ],
        out_specs=pl.BlockSpec((t, LRU_WIDTH), lambda b, c: (b * nc + c, 0)),
        out_shape=jax.ShapeDtypeStruct((m, LRU_WIDTH), BF16),
        scratch_shapes=[
            pltpu.VMEM((t + SUBLANES, LRU_WIDTH), F32),
            pltpu.VMEM((SUBLANES, LRU_WIDTH), F32),
            pltpu.VMEM((t, LRU_WIDTH), F32),
            pltpu.VMEM((t, LRU_WIDTH), F32),
        ],
        compiler_params=_params("parallel", "arbitrary"),
        name="rg_lru",
    )(z0, z0, conv_w, conv_b.reshape(1, -1), gx_w.astype(BF16), gx_b.reshape(1, -1),
      ga_w.astype(BF16), ga_b.reshape(1, -1), lam.reshape(1, -1))


def _swa_body(sink_ref, q_ref, kp_ref, kc_ref, vp_ref, vc_ref, gate_ref, o_ref):
    n = pl.program_id(1)
    blk = SWA_BLOCK
    qi = lax.broadcasted_iota(jnp.int32, (blk, 2 * blk), 0)
    ki = lax.broadcasted_iota(jnp.int32, (blk, 2 * blk), 1)
    rel = qi + blk - ki
    mask = (rel >= 0) & (rel < blk) & ((ki >= blk) | (n > 0))
    k2 = jnp.concatenate([kp_ref[...], kc_ref[...]], axis=0)
    v2 = jnp.concatenate([vp_ref[...], vc_ref[...]], axis=0)
    scale = SWA_HEAD_DIM ** -0.5
    for g in range(SWA_KV_HEADS):
        kg = k2[:, g * SWA_HEAD_DIM:(g + 1) * SWA_HEAD_DIM]
        vg = v2[:, g * SWA_HEAD_DIM:(g + 1) * SWA_HEAD_DIM]
        for hh in range(SWA_GROUP):
            h = g * SWA_GROUP + hh
            c0 = h * SWA_HEAD_DIM
            qh = q_ref[:, c0:c0 + SWA_HEAD_DIM]
            s = lax.dot_general(qh, kg, (((1,), (1,)), ((), ())),
                                preferred_element_type=F32) * scale
            s = jnp.where(mask, s, -jnp.inf)
            sink = sink_ref[h]
            mx = jnp.maximum(jnp.max(s, axis=-1, keepdims=True), sink)
            p = jnp.exp(s - mx)
            den = jnp.sum(p, axis=-1, keepdims=True) + jnp.exp(sink - mx)
            p = p / den
            o = jnp.dot(p.astype(BF16), vg, preferred_element_type=F32)
            gt = gate_ref[:, c0:c0 + SWA_HEAD_DIM].astype(F32)
            o_ref[:, c0:c0 + SWA_HEAD_DIM] = (o * _silu(gt)).astype(o_ref.dtype)


def _swa(z0, sinks, *, batch, seq):
    m = batch * seq
    nb = seq // SWA_BLOCK
    wide = lambda col: pl.BlockSpec((SWA_BLOCK, SWA_WIDTH), lambda b, n: (b * nb + n, col))
    cur = lambda col: pl.BlockSpec((SWA_BLOCK, SWA_KV_WIDTH), lambda b, n: (b * nb + n, col))
    prev = lambda col: pl.BlockSpec(
        (SWA_BLOCK, SWA_KV_WIDTH), lambda b, n: (b * nb + jnp.maximum(n - 1, 0), col))
    return pl.pallas_call(
        _swa_body,
        grid=(batch, nb),
        in_specs=[
            pl.BlockSpec(memory_space=pltpu.SMEM),
            wide(Z0_Q // SWA_WIDTH),
            prev(Z0_K // SWA_KV_WIDTH), cur(Z0_K // SWA_KV_WIDTH),
            prev(Z0_V // SWA_KV_WIDTH), cur(Z0_V // SWA_KV_WIDTH),
            wide(Z0_SWA_GATE // SWA_WIDTH),
        ],
        out_specs=pl.BlockSpec((SWA_BLOCK, SWA_WIDTH), lambda b, n: (b * nb + n, 0)),
        out_shape=jax.ShapeDtypeStruct((m, SWA_WIDTH), BF16),
        compiler_params=_params("parallel", "parallel"),
        name="swa",
    )(sinks, z0, z0, z0, z0, z0, z0)


def _rope128(x, cos2, sin2):
    lane = lax.broadcasted_iota(jnp.int32, x.shape, 1)
    swapped = jnp.where(lane < MLA_ROPE // 2,
                        pltpu.roll(x, LANES - MLA_ROPE // 2, axis=1),
                        pltpu.roll(x, MLA_ROPE // 2, axis=1))
    return x * cos2 + swapped * sin2


def _mla_up_body(cq_ref, ckv_ref, kr_ref, qn_ref, kvn_ref, wq_ref, wkv_ref, cos_ref, sin_ref,
                 q_ref, k_ref, v_ref):
    cos2 = cos_ref[...]
    sin2 = sin_ref[...]

    def normed(x, g):
        ms = jnp.mean(x * x, axis=-1, keepdims=True)
        return (x * lax.rsqrt(ms + NORM_EPS) * g).astype(BF16)

    hq = normed(cq_ref[:, :MLA_Q_RANK].astype(F32), qn_ref[...])
    hkv = normed(ckv_ref[...].astype(F32), kvn_ref[...])
    lane = lax.broadcasted_iota(jnp.int32, kr_ref.shape, 1)
    kr = jnp.where(lane < MLA_ROPE, kr_ref[...].astype(F32), 0.0)
    kr = _rope128(kr, cos2, sin2).astype(BF16)
    scale = (MLA_NOPE + MLA_ROPE) ** -0.5 * math.log2(math.e)
    for h in range(MLA_HEADS):
        c0 = h * MLA_QK_PAD
        qh = jnp.dot(hq, wq_ref[:, c0:c0 + MLA_QK_PAD], preferred_element_type=F32)
        q_ref[:, c0:c0 + MLA_NOPE] = (qh[:, :MLA_NOPE] * scale).astype(BF16)
        q_ref[:, c0 + MLA_NOPE:c0 + MLA_QK_PAD] = (
            _rope128(qh[:, MLA_NOPE:], cos2, sin2) * scale).astype(BF16)
        kvh = jnp.dot(hkv, wkv_ref[:, c0:c0 + MLA_NOPE + MLA_V], preferred_element_type=F32)
        k_ref[:, c0:c0 + MLA_NOPE] = kvh[:, :MLA_NOPE].astype(BF16)
        k_ref[:, c0 + MLA_NOPE:c0 + MLA_QK_PAD] = kr
        v_ref[:, h * MLA_V:(h + 1) * MLA_V] = kvh[:, MLA_NOPE:].astype(BF16)


def _mla_up(z1, q_norm, kv_norm, wq_pad, wkv, cos2, sin2, *, seq, tm):
    m = z1.shape[0]
    npos = seq // tm
    full = lambda shape: pl.BlockSpec(shape, lambda i: (0,) * len(shape))
    return pl.pallas_call(
        _mla_up_body,
        grid=(m // tm,),
        in_specs=[
            pl.BlockSpec((tm, 1024), lambda i: (i, Z1_CQ // 1024)),
            pl.BlockSpec((tm, MLA_KV_RANK), lambda i: (i, Z1_CKV // MLA_KV_RANK)),
            pl.BlockSpec((tm, LANES), lambda i: (i, Z1_KROPE // LANES)),
            full((1, MLA_Q_RANK)),
            full((1, MLA_KV_RANK)),
            full((MLA_Q_RANK, MLA_HEADS * MLA_QK_PAD)),
            full((MLA_KV_RANK, MLA_HEADS * (MLA_NOPE + MLA_V))),
            pl.BlockSpec((tm, LANES), lambda i: (i % npos, 0)),
            pl.BlockSpec((tm, LANES), lambda i: (i % npos, 0)),
        ],
        out_specs=[
            pl.BlockSpec((tm, MLA_HEADS * MLA_QK_PAD), lambda i: (i, 0)),
            pl.BlockSpec((tm, MLA_HEADS * MLA_QK_PAD), lambda i: (i, 0)),
            pl.BlockSpec((tm, MLA_WIDTH), lambda i: (i, 0)),
        ],
        out_shape=[
            jax.ShapeDtypeStruct((m, MLA_HEADS * MLA_QK_PAD), BF16),
            jax.ShapeDtypeStruct((m, MLA_HEADS * MLA_QK_PAD), BF16),
            jax.ShapeDtypeStruct((m, MLA_WIDTH), BF16),
        ],
        compiler_params=_params("parallel"),
        name="mla_up",
    )(z1, z1, z1, q_norm.reshape(1, -1), kv_norm.reshape(1, -1), wq_pad, wkv, cos2, sin2)


def _flash_t(scores, vt, n_full, mask, s_a, s_b, m_scr, l_scr, acc_scr):
    m_scr[...] = jnp.full(m_scr.shape, -jnp.inf, F32)
    l_scr[...] = jnp.zeros(l_scr.shape, F32)
    acc_scr[...] = jnp.zeros(acc_scr.shape, F32)

    def softmax_pv(s_ref, j, msk):
        st = s_ref[...]
        if msk is not None:
            st = jnp.where(msk, st, -jnp.inf)
        m_prev = m_scr[...]
        m_new = jnp.maximum(m_prev, jnp.max(st, axis=0, keepdims=True))
        pt = jnp.exp2(st - m_new)
        alpha = jnp.exp2(m_prev - m_new)
        l_scr[...] = alpha * l_scr[...] + jnp.sum(pt, axis=0, keepdims=True)
        acc_scr[...] = alpha * acc_scr[...] + jnp.dot(
            vt(j), pt.astype(BF16), preferred_element_type=F32)
        m_scr[...] = m_new

    s_a[...] = scores(0)

    def pair(jj, carry):
        j0 = 2 * jj
        s_b[...] = scores(j0 + 1)
        softmax_pv(s_a, j0, None)
        s_a[...] = scores(j0 + 2)
        softmax_pv(s_b, j0 + 1, None)
        return carry

    lax.fori_loop(0, n_full // 2, pair, 0)

    @pl.when(n_full % 2 == 1)
    def _():
        s_b[...] = scores(n_full)
        softmax_pv(s_a, n_full - 1, None)
        softmax_pv(s_b, n_full, mask)

    @pl.when(n_full % 2 == 0)
    def _():
        softmax_pv(s_a, n_full, mask)


def _flash_scratch(bk, nq, dv, nchunk):
    return [
        pltpu.VMEM((nchunk, dv, bk), BF16),
        pltpu.VMEM((bk, nq), F32),
        pltpu.VMEM((bk, nq), F32),
        pltpu.VMEM((1, nq), F32),
        pltpu.VMEM((1, nq), F32),
        pltpu.VMEM((dv, nq), F32),
    ]


def _store_vt(v_ref, vt_scr, bk):
    for c in range(vt_scr.shape[0]):
        vt_scr[c] = v_ref[c * bk:(c + 1) * bk, :].T


def _mla_attn_body(q_ref, k_ref, v_ref, gate_ref, o_ref,
                   vt_scr, s_a, s_b, m_scr, l_scr, acc_scr, *, bq):
    i = pl.program_id(2)

    @pl.when(i == 0)
    def _():
        _store_vt(v_ref, vt_scr, bq)

    qt = q_ref[...].T

    def scores(j):
        r0 = pl.multiple_of(j * bq, bq)
        return jnp.dot(k_ref[pl.ds(r0, bq), :], qt, preferred_element_type=F32)

    key = lax.broadcasted_iota(jnp.int32, (bq, bq), 0)
    qry = lax.broadcasted_iota(jnp.int32, (bq, bq), 1)
    _flash_t(scores, lambda j: vt_scr[j], i, key <= qry, s_a, s_b, m_scr, l_scr, acc_scr)
    g = gate_ref[...].astype(F32)
    o_ref[...] = ((acc_scr[...] / l_scr[...]).T * _silu(g)).astype(o_ref.dtype)


def _mla_attn(q, k, v, z1, *, batch, seq, bq):
    m = batch * seq
    nq = seq // bq
    return pl.pallas_call(
        functools.partial(_mla_attn_body, bq=bq),
        grid=(batch, MLA_HEADS, nq),
        in_specs=[
            pl.BlockSpec((bq, MLA_QK_PAD), lambda b, h, i: (b * nq + i, h)),
            pl.BlockSpec((seq, MLA_QK_PAD), lambda b, h, i: (b, h)),
            pl.BlockSpec((seq, MLA_V), lambda b, h, i: (b, h)),
            pl.BlockSpec((bq, MLA_V), lambda b, h, i: (b * nq + i, Z1_MLA_GATE // MLA_V + h)),
        ],
        out_specs=pl.BlockSpec((bq, MLA_V), lambda b, h, i: (b * nq + i, h)),
        out_shape=jax.ShapeDtypeStruct((m, MLA_WIDTH), BF16),
        scratch_shapes=_flash_scratch(bq, bq, MLA_V, nq),
        compiler_params=_params("parallel", "parallel", "arbitrary"),
        name="mla_attn",
    )(q, k, v, z1)


def _diff_attn_body(q_ref, k_ref, v_ref, gate_ref, lq1_ref, lk1_ref, lq2_ref, lk2_ref, sub_ref,
                    o_ref, vt_scr, s_a, s_b, m_scr, l_scr, acc_scr, *, bq):
    i = pl.program_id(2)

    @pl.when(i == 0)
    def _():
        _store_vt(v_ref, vt_scr, bq)

    q = q_ref[...].astype(F32) * (DIFF_QK ** -0.5 * math.log2(math.e))
    lane = lax.broadcasted_iota(jnp.int32, q.shape, 1)
    q1 = jnp.where(lane < DIFF_QK, q, 0.0).astype(BF16)
    q2 = jnp.where(lane >= DIFF_QK, q, 0.0).astype(BF16)
    qt = jnp.concatenate([q1.T, q2.T], axis=1)

    def scores(j):
        r0 = pl.multiple_of(j * bq, bq)
        return jnp.dot(k_ref[pl.ds(r0, bq), :], qt, preferred_element_type=F32)

    key = lax.broadcasted_iota(jnp.int32, (bq, 2 * bq), 0)
    qry = lax.broadcasted_iota(jnp.int32, (bq, 2 * bq), 1)
    qry = jnp.where(qry >= bq, qry - bq, qry)
    _flash_t(scores, lambda j: vt_scr[j], i, key <= qry, s_a, s_b, m_scr, l_scr, acc_scr)

    lam = (jnp.exp(jnp.sum(lq1_ref[...] * lk1_ref[...], axis=-1, keepdims=True))
           - jnp.exp(jnp.sum(lq2_ref[...] * lk2_ref[...], axis=-1, keepdims=True))
           + DIFF_LAMBDA_INIT)
    ot = acc_scr[...] / l_scr[...]
    od = (ot[:, :bq] - lam * ot[:, bq:]).T
    ms = jnp.mean(od * od, axis=-1, keepdims=True)
    od = od * lax.rsqrt(ms + NORM_EPS) * sub_ref[...] * (1.0 - DIFF_LAMBDA_INIT)
    g = gate_ref[...].astype(F32)
    o_ref[...] = (od * _silu(g)).astype(o_ref.dtype)


def _diff_attn(z1, lq1, lk1, lq2, lk2, subln, *, batch, seq, bq):
    m = batch * seq
    nq = seq // bq
    vec = lambda n: pl.BlockSpec((1, n), lambda b, h, i: (0, 0))
    return pl.pallas_call(
        functools.partial(_diff_attn_body, bq=bq),
        grid=(batch, DIFF_HEADS, nq),
        in_specs=[
            pl.BlockSpec((bq, DIFF_V), lambda b, h, i: (b * nq + i, Z1_DQ // DIFF_V + h)),
            pl.BlockSpec((seq, DIFF_V), lambda b, h, i: (b, Z1_DK // DIFF_V + h)),
            pl.BlockSpec((seq, DIFF_V), lambda b, h, i: (b, Z1_DV // DIFF_V + h)),
            pl.BlockSpec((bq, DIFF_V), lambda b, h, i: (b * nq + i, Z1_DIFF_GATE // DIFF_V + h)),
            vec(DIFF_QK), vec(DIFF_QK), vec(DIFF_QK), vec(DIFF_QK), vec(DIFF_V),
        ],
        out_specs=pl.BlockSpec((bq, DIFF_V), lambda b, h, i: (b * nq + i, h)),
        out_shape=jax.ShapeDtypeStruct((m, DIFF_WIDTH), BF16),
        scratch_shapes=_flash_scratch(bq, 2 * bq, DIFF_V, nq),
        compiler_params=_params("parallel", "parallel", "arbitrary"),
        name="diff_attn",
    )(z1, z1, z1, z1, lq1.reshape(1, -1), lk1.reshape(1, -1), lq2.reshape(1, -1),
      lk2.reshape(1, -1), subln.reshape(1, -1))


def _layer0_w_in(w):
    q0 = 2 * LRU_WIDTH
    k0 = q0 + SWA_WIDTH
    g0 = k0 + 2 * SWA_KV_WIDTH
    return jnp.concatenate([w[:, :k0], w[:, g0:], w[:, k0:g0]], axis=1).astype(BF16)


def _layer1_w_in(w):
    ckv0 = MLA_Q_RANK
    kr0 = ckv0 + MLA_KV_RANK
    rest0 = kr0 + MLA_ROPE
    pad = jnp.zeros((w.shape[0], Z1_CKV - Z1_KROPE - MLA_ROPE), w.dtype)
    return jnp.concatenate(
        [w[:, :ckv0], w[:, kr0:rest0], pad, w[:, ckv0:kr0], w[:, rest0:]], axis=1).astype(BF16)


def _pad_w_uq(w):
    w = w.reshape(MLA_Q_RANK, MLA_HEADS, MLA_NOPE + MLA_ROPE)
    w = jnp.pad(w, ((0, 0), (0, 0), (0, MLA_QK_PAD - MLA_NOPE - MLA_ROPE)))
    return w.reshape(MLA_Q_RANK, MLA_HEADS * MLA_QK_PAD).astype(BF16)


def _rope_tables(seq):
    half = MLA_ROPE // 2
    freq = ROPE_THETA ** (-jnp.arange(half, dtype=F32) / half)
    ang = jnp.arange(seq).astype(F32)[:, None] * freq[None, :]
    cos, sin = jnp.cos(ang), jnp.sin(ang)
    zeros = jnp.zeros((seq, LANES - MLA_ROPE), F32)
    return (jnp.concatenate([cos, cos, zeros], axis=1),
            jnp.concatenate([-sin, sin, zeros], axis=1))


def kernel(x, norm_gains, final_norm_gain, l0_w_in, l0_conv_w, l0_conv_b, l0_gate_x_w, l0_gate_x_b, l0_gate_a_w, l0_gate_a_b, l0_lru_lambda, l0_sinks, l0_w_out, l1_w_in, l1_q_norm, l1_w_uq, l1_kv_norm, l1_w_ukv, l1_lambda_q1, l1_lambda_k1, l1_lambda_q2, l1_lambda_k2, l1_subln, l1_w_out):
    batch, seq, d = x.shape
    m = batch * seq
    x0 = x.reshape(m, d)
    tm = min(512, seq)
    bq = min(512, seq)
    t_lru = min(256, seq)

    z0 = _norm_matmul(x0, norm_gains[0], _layer0_w_in(l0_w_in),
                      tm=tm, n_blocks=1, chunk=512, name="in_proj0")
    y_a = _lru(z0, l0_conv_w, l0_conv_b, l0_gate_x_w, l0_gate_x_b, l0_gate_a_w, l0_gate_a_b,
               l0_lru_lambda, batch=batch, seq=seq, t=t_lru)
    y_b = _swa(z0, l0_sinks, batch=batch, seq=seq)
    x1 = _out_proj(y_a, y_b, l0_w_out.astype(BF16), x0, final_norm_gain,
                   tm=tm, chunk=512, final_norm=False, name="out_proj0")

    z1 = _norm_matmul(x1, norm_gains[1], _layer1_w_in(l1_w_in),
                      tm=tm, n_blocks=2, chunk=512, name="in_proj1")
    cos2, sin2 = _rope_tables(seq)
    q, k, v = _mla_up(z1, l1_q_norm, l1_kv_norm, _pad_w_uq(l1_w_uq), l1_w_ukv.astype(BF16),
                      cos2, sin2, seq=seq, tm=tm)
    y_c = _mla_attn(q, k, v, z1, batch=batch, seq=seq, bq=bq)
    y_d = _diff_attn(z1, l1_lambda_q1, l1_lambda_k1, l1_lambda_q2, l1_lambda_k2, l1_subln,
                     batch=batch, seq=seq, bq=bq)
    out = _out_proj(y_c, y_d, l1_w_out.astype(BF16), x1, final_norm_gain,
                    tm=tm, chunk=512, final_norm=True, name="out_proj1")
    return out.reshape(batch, seq, d)
```

```python
import functools
import math

import jax
import jax.numpy as jnp
from jax import lax
from jax.experimental import pallas as pl
from jax.experimental.pallas import tpu as pltpu

F32 = jnp.float32
BF16 = jnp.bfloat16

D_MODEL = 2048
NORM_EPS = 1e-6
LANES = 128
SUBLANES = 8
VMEM_LIMIT = 56 * 1024 * 1024

LRU_WIDTH = 1024
LRU_BLOCKS = 8
LRU_BLOCK_DIM = LRU_WIDTH // LRU_BLOCKS
CONV_WIDTH = 4
LRU_C = 8.0

SWA_Q_HEADS = 16
SWA_KV_HEADS = 2
SWA_HEAD_DIM = 64
SWA_GROUP = SWA_Q_HEADS // SWA_KV_HEADS
SWA_WIDTH = SWA_Q_HEADS * SWA_HEAD_DIM
SWA_KV_WIDTH = SWA_KV_HEADS * SWA_HEAD_DIM
SWA_BLOCK = 128

MLA_HEADS = 8
MLA_Q_RANK = 768
MLA_KV_RANK = 512
MLA_NOPE = 128
MLA_ROPE = 64
MLA_V = 128
MLA_QK_PAD = 256
MLA_WIDTH = MLA_HEADS * MLA_V
ROPE_THETA = 10000.0

DIFF_HEADS = 8
DIFF_QK = 64
DIFF_V = 2 * DIFF_QK
DIFF_WIDTH = DIFF_HEADS * DIFF_V
DIFF_LAMBDA_INIT = 0.8 - 0.6 * math.exp(-0.3 * 1)

Z1_CQ = 0
Z1_KROPE = 768
Z1_CKV = 1024
Z1_MLA_GATE = 1536
Z1_DQ = 2560
Z1_DK = 3584
Z1_DV = 4608
Z1_DIFF_GATE = 5632
Z1_WIDTH = 6656

Z0_LRU_X = 0
Z0_LRU_GATE = 1024
Z0_Q = 2048
Z0_SWA_GATE = 3072
Z0_K = 4096
Z0_V = 4224
Z0_WIDTH = 4352


def _params(*sem):
    return pltpu.CompilerParams(dimension_semantics=sem, vmem_limit_bytes=VMEM_LIMIT)


def _silu(g):
    return g * jax.nn.sigmoid(g)


def _col_chunks(n, width):
    out, c = [], 0
    while c < n:
        w = min(width, n - c)
        out.append((c, w))
        c += w
    return out


def _norm_matmul_body(x_ref, g_ref, w_ref, o_ref, *, chunk):
    x = x_ref[...]
    ms = jnp.mean(x * x, axis=-1, keepdims=True)
    h = (x * lax.rsqrt(ms + NORM_EPS) * g_ref[...]).astype(BF16)
    for c0, cw in _col_chunks(w_ref.shape[1], chunk):
        o_ref[:, c0:c0 + cw] = jnp.dot(
            h, w_ref[:, c0:c0 + cw], preferred_element_type=F32).astype(o_ref.dtype)


def _norm_matmul(x, g, w, *, tm, n_blocks, chunk, name):
    m, d = x.shape
    n = w.shape[1]
    tn = n // n_blocks
    w_mode = dict(pipeline_mode=pl.Buffered(1)) if n_blocks == 1 else {}
    return pl.pallas_call(
        functools.partial(_norm_matmul_body, chunk=chunk),
        grid=(n_blocks, m // tm),
        in_specs=[
            pl.BlockSpec((tm, d), lambda j, i: (i, 0)),
            pl.BlockSpec((1, d), lambda j, i: (0, 0)),
            pl.BlockSpec((d, tn), lambda j, i: (0, j), **w_mode),
        ],
        out_specs=pl.BlockSpec((tm, tn), lambda j, i: (i, j)),
        out_shape=jax.ShapeDtypeStruct((m, n), BF16),
        compiler_params=_params("parallel", "parallel"),
        name=name,
    )(x, g.reshape(1, d), w)


def _out_proj_body(ya_ref, yb_ref, w_ref, x_ref, g_ref, o_ref, *, chunk, final_norm):
    ya = ya_ref[...]
    yb = yb_ref[...]
    ka = ya.shape[1]
    ssq = None
    for c0, cw in _col_chunks(w_ref.shape[1], chunk):
        r = jnp.dot(ya, w_ref[:ka, c0:c0 + cw], preferred_element_type=F32)
        r = r + jnp.dot(yb, w_ref[ka:, c0:c0 + cw], preferred_element_type=F32)
        r = r + x_ref[:, c0:c0 + cw]
        o_ref[:, c0:c0 + cw] = r
        if final_norm:
            part = jnp.sum(r * r, axis=-1, keepdims=True)
            ssq = part if ssq is None else ssq + part
    if final_norm:
        inv = lax.rsqrt(ssq / w_ref.shape[1] + NORM_EPS)
        o_ref[...] = o_ref[...] * inv * g_ref[...]


def _out_proj(ya, yb, w, x, g, *, tm, chunk, final_norm, name):
    m, d = x.shape
    ka, kb = ya.shape[1], yb.shape[1]
    return pl.pallas_call(
        functools.partial(_out_proj_body, chunk=chunk, final_norm=final_norm),
        grid=(m // tm,),
        in_specs=[
            pl.BlockSpec((tm, ka), lambda i: (i, 0)),
            pl.BlockSpec((tm, kb), lambda i: (i, 0)),
            pl.BlockSpec((ka + kb, d), lambda i: (0, 0)),
            pl.BlockSpec((tm, d), lambda i: (i, 0)),
            pl.BlockSpec((1, d), lambda i: (0, 0)),
        ],
        out_specs=pl.BlockSpec((tm, d), lambda i: (i, 0)),
        out_shape=jax.ShapeDtypeStruct((m, d), F32),
        compiler_params=_params("parallel"),
        name=name,
    )(ya, yb, w, x, g.reshape(1, d))


def _lru_body(x_ref, gate_ref, cw_ref, cb_ref, gxw_ref, gxb_ref, gaw_ref, gab_ref, lam_ref,
              o_ref, xbuf, hcar, a_scr, b_scr, *, t):
    c = pl.program_id(1)

    @pl.when(c == 0)
    def _():
        xbuf[0:SUBLANES, :] = jnp.zeros((SUBLANES, LRU_WIDTH), F32)
        hcar[...] = jnp.zeros((SUBLANES, LRU_WIDTH), F32)

    xbuf[SUBLANES:SUBLANES + t, :] = x_ref[...].astype(F32)
    xc = cb_ref[...] + cw_ref[0:1, :] * xbuf[SUBLANES - 3:SUBLANES - 3 + t, :]
    for k in range(1, CONV_WIDTH):
        r0 = SUBLANES - 3 + k
        xc = xc + cw_ref[k:k + 1, :] * xbuf[r0:r0 + t, :]
    xbuf[0:SUBLANES, :] = xbuf[t:t + SUBLANES, :]

    xcb = xc.astype(BF16)
    gi_parts, gr_parts = [], []
    for n in range(LRU_BLOCKS):
        xb = xcb[:, n * LRU_BLOCK_DIM:(n + 1) * LRU_BLOCK_DIM]
        gi_parts.append(jnp.dot(xb, gxw_ref[n], preferred_element_type=F32))
        gr_parts.append(jnp.dot(xb, gaw_ref[n], preferred_element_type=F32))
    gate_i = jax.nn.sigmoid(jnp.concatenate(gi_parts, axis=-1) + gxb_ref[...])
    gate_r = jax.nn.sigmoid(jnp.concatenate(gr_parts, axis=-1) + gab_ref[...])

    neg_lam = -lam_ref[...]
    softplus = jnp.maximum(neg_lam, 0.0) + jnp.log1p(jnp.exp(-jnp.abs(neg_lam)))
    log_a = (-LRU_C * softplus) * gate_r
    a = jnp.exp(log_a)
    mult = jnp.sqrt(1.0 - a * a)
    u = mult * gate_i * xc

    row = lax.broadcasted_iota(jnp.int32, (t, LRU_WIDTH), 0) % SUBLANES
    d = 1
    while d < SUBLANES:
        a_sh = pltpu.roll(a, d, axis=0)
        u_sh = pltpu.roll(u, d, axis=0)
        ok = row >= d
        u = jnp.where(ok, a * u_sh + u, u)
        a = jnp.where(ok, a * a_sh, a)
        d *= 2
    a_scr[...] = a
    b_scr[...] = u

    def group(gidx, h_prev):
        r0 = pl.multiple_of(gidx * SUBLANES, SUBLANES)
        hg = a_scr[pl.ds(r0, SUBLANES), :] * h_prev + b_scr[pl.ds(r0, SUBLANES), :]
        b_scr[pl.ds(r0, SUBLANES), :] = hg
        return jnp.broadcast_to(hg[SUBLANES - 1:SUBLANES, :], (SUBLANES, LRU_WIDTH))

    hcar[...] = lax.fori_loop(0, t // SUBLANES, group, hcar[...])

    g = gate_ref[...].astype(F32)
    o_ref[...] = (b_scr[...] * _silu(g)).astype(o_ref.dtype)


def _lru(z0, conv_w, conv_b, gx_w, gx_b, ga_w, ga_b, lam, *, batch, seq, t):
    m = batch * seq
    nc = seq // t
    row_spec = lambda col: pl.BlockSpec((t, LRU_WIDTH), lambda b, c: (b * nc + c, col))
    full = lambda shape: pl.BlockSpec(shape, lambda b, c: (0,) * len(shape))
    return pl.pallas_call(
        functools.partial(_lru_body, t=t),
        grid=(batch, nc),
        in_specs=[
            row_spec(Z0_LRU_X // LRU_WIDTH),
            row_spec(Z0_LRU_GATE // LRU_WIDTH),
            full((CONV_WIDTH, LRU_WIDTH)),
            full((1, LRU_WIDTH)),
            full((LRU_BLOCKS, LRU_BLOCK_DIM, LRU_BLOCK_DIM)),
            full((1, LRU_WIDTH)),
            full((LRU_BLOCKS, LRU_BLOCK_DIM, LRU_BLOCK_DIM)),
            full((1, LRU_WIDTH)),
            full((1, LRU_WIDTH)),
        ],
        out_specs=pl.BlockSpec((t, LRU_WIDTH), lambda b, c: (b * nc + c, 0)),
        out_shape=jax.ShapeDtypeStruct((m, LRU_WIDTH), BF16),
        scratch_shapes=[
            pltpu.VMEM((t + SUBLANES, LRU_WIDTH), F32),
            pltpu.VMEM((SUBLANES, LRU_WIDTH), F32),
            pltpu.VMEM((t, LRU_WIDTH), F32),
            pltpu.VMEM((t, LRU_WIDTH), F32),
        ],
        compiler_params=_params("parallel", "arbitrary"),
        name="rg_lru",
    )(z0, z0, conv_w, conv_b.reshape(1, -1), gx_w.astype(BF16), gx_b.reshape(1, -1),
      ga_w.astype(BF16), ga_b.reshape(1, -1), lam.reshape(1, -1))


def _swa_body(sink_ref, q_ref, kp_ref, kc_ref, vp_ref, vc_ref, gate_ref, o_ref):
    n = pl.program_id(1)
    blk = SWA_BLOCK
    qi = lax.broadcasted_iota(jnp.int32, (blk, 2 * blk), 0)
    ki = lax.broadcasted_iota(jnp.int32, (blk, 2 * blk), 1)
    rel = qi + blk - ki
    mask = (rel >= 0) & (rel < blk) & ((ki >= blk) | (n > 0))
    k2 = jnp.concatenate([kp_ref[...], kc_ref[...]], axis=0)
    v2 = jnp.concatenate([vp_ref[...], vc_ref[...]], axis=0)
    scale = SWA_HEAD_DIM ** -0.5
    for g in range(SWA_KV_HEADS):
        kg = k2[:, g * SWA_HEAD_DIM:(g + 1) * SWA_HEAD_DIM]
        vg = v2[:, g * SWA_HEAD_DIM:(g + 1) * SWA_HEAD_DIM]
        for hh in range(SWA_GROUP):
            h = g * SWA_GROUP + hh
            c0 = h * SWA_HEAD_DIM
            qh = q_ref[:, c0:c0 + SWA_HEAD_DIM]
            s = lax.dot_general(qh, kg, (((1,), (1,)), ((), ())),
                                preferred_element_type=F32) * scale
            s = jnp.where(mask, s, -jnp.inf)
            sink = sink_ref[h]
            mx = jnp.maximum(jnp.max(s, axis=-1, keepdims=True), sink)
            p = jnp.exp(s - mx)
            den = jnp.sum(p, axis=-1, keepdims=True) + jnp.exp(sink - mx)
            p = p / den
            o = jnp.dot(p.astype(BF16), vg, preferred_element_type=F32)
            gt = gate_ref[:, c0:c0 + SWA_HEAD_DIM].astype(F32)
            o_ref[:, c0:c0 + SWA_HEAD_DIM] = (o * _silu(gt)).astype(o_ref.dtype)


def _swa(z0, sinks, *, batch, seq):
    m = batch * seq
    nb = seq // SWA_BLOCK
    wide = lambda col: pl.BlockSpec((SWA_BLOCK, SWA_WIDTH), lambda b, n: (b * nb + n, col))
    cur = lambda col: pl.BlockSpec((SWA_BLOCK, SWA_KV_WIDTH), lambda b, n: (b * nb + n, col))
    prev = lambda col: pl.BlockSpec(
        (SWA_BLOCK, SWA_KV_WIDTH), lambda b, n: (b * nb + jnp.maximum(n - 1, 0), col))
    return pl.pallas_call(
        _swa_body,
        grid=(batch, nb),
        in_specs=[
            pl.BlockSpec(memory_space=pltpu.SMEM),
            wide(Z0_Q // SWA_WIDTH),
            prev(Z0_K // SWA_KV_WIDTH), cur(Z0_K // SWA_KV_WIDTH),
            prev(Z0_V // SWA_KV_WIDTH), cur(Z0_V // SWA_KV_WIDTH),
            wide(Z0_SWA_GATE // SWA_WIDTH),
        ],
        out_specs=pl.BlockSpec((SWA_BLOCK, SWA_WIDTH), lambda b, n: (b * nb + n, 0)),
        out_shape=jax.ShapeDtypeStruct((m, SWA_WIDTH), BF16),
        compiler_params=_params("parallel", "parallel"),
        name="swa",
    )(sinks, z0, z0, z0, z0, z0, z0)


def _rope128(x, cos2, sin2):
    lane = lax.broadcasted_iota(jnp.int32, x.shape, 1)
    swapped = jnp.where(lane < MLA_ROPE // 2,
                        pltpu.roll(x, LANES - MLA_ROPE // 2, axis=1),
                        pltpu.roll(x, MLA_ROPE // 2, axis=1))
    return x * cos2 + swapped * sin2


def _mla_up_body(cq_ref, ckv_ref, kr_ref, qn_ref, kvn_ref, wq_ref, wkv_ref, cos_ref, sin_ref,
                 q_ref, k_ref, v_ref):
    cos2 = cos_ref[...]
    sin2 = sin_ref[...]

    def normed(x, g):
        ms = jnp.mean(x * x, axis=-1, keepdims=True)
        return (x * lax.rsqrt(ms + NORM_EPS) * g).astype(BF16)

    hq = normed(cq_ref[:, :MLA_Q_RANK].astype(F32), qn_ref[...])
    hkv = normed(ckv_ref[...].astype(F32), kvn_ref[...])
    lane = lax.broadcasted_iota(jnp.int32, kr_ref.shape, 1)
    kr = jnp.where(lane < MLA_ROPE, kr_ref[...].astype(F32), 0.0)
    kr = _rope128(kr, cos2, sin2).astype(BF16)
    scale = (MLA_NOPE + MLA_ROPE) ** -0.5 * math.log2(math.e)
    for h in range(MLA_HEADS):
        c0 = h * MLA_QK_PAD
        qh = jnp.dot(hq, wq_ref[:, c0:c0 + MLA_QK_PAD], preferred_element_type=F32)
        q_ref[:, c0:c0 + MLA_NOPE] = (qh[:, :MLA_NOPE] * scale).astype(BF16)
        q_ref[:, c0 + MLA_NOPE:c0 + MLA_QK_PAD] = (
            _rope128(qh[:, MLA_NOPE:], cos2, sin2) * scale).astype(BF16)
        kvh = jnp.dot(hkv, wkv_ref[:, c0:c0 + MLA_NOPE + MLA_V], preferred_element_type=F32)
        k_ref[:, c0:c0 + MLA_NOPE] = kvh[:, :MLA_NOPE].astype(BF16)
        k_ref[:, c0 + MLA_NOPE:c0 + MLA_QK_PAD] = kr
        v_ref[:, h * MLA_V:(h + 1) * MLA_V] = kvh[:, MLA_NOPE:].astype(BF16)


def _mla_up(z1, q_norm, kv_norm, wq_pad, wkv, cos2, sin2, *, seq, tm):
    m = z1.shape[0]
    npos = seq // tm
    full = lambda shape: pl.BlockSpec(shape, lambda i: (0,) * len(shape))
    return pl.pallas_call(
        _mla_up_body,
        grid=(m // tm,),
        in_specs=[
            pl.BlockSpec((tm, 1024), lambda i: (i, Z1_CQ // 1024)),
            pl.BlockSpec((tm, MLA_KV_RANK), lambda i: (i, Z1_CKV // MLA_KV_RANK)),
            pl.BlockSpec((tm, LANES), lambda i: (i, Z1_KROPE // LANES)),
            full((1, MLA_Q_RANK)),
            full((1, MLA_KV_RANK)),
            full((MLA_Q_RANK, MLA_HEADS * MLA_QK_PAD)),
            full((MLA_KV_RANK, MLA_HEADS * (MLA_NOPE + MLA_V))),
            pl.BlockSpec((tm, LANES), lambda i: (i % npos, 0)),
            pl.BlockSpec((tm, LANES), lambda i: (i % npos, 0)),
        ],
        out_specs=[
            pl.BlockSpec((tm, MLA_HEADS * MLA_QK_PAD), lambda i: (i, 0)),
            pl.BlockSpec((tm, MLA_HEADS * MLA_QK_PAD), lambda i: (i, 0)),
            pl.BlockSpec((tm, MLA_WIDTH), lambda i: (i, 0)),
        ],
        out_shape=[
            jax.ShapeDtypeStruct((m, MLA_HEADS * MLA_QK_PAD), BF16),
            jax.ShapeDtypeStruct((m, MLA_HEADS * MLA_QK_PAD), BF16),
            jax.ShapeDtypeStruct((m, MLA_WIDTH), BF16),
        ],
        compiler_params=_params("parallel"),
        name="mla_up",
    )(z1, z1, z1, q_norm.reshape(1, -1), kv_norm.reshape(1, -1), wq_pad, wkv, cos2, sin2)


def _flash_t(scores, vt, n_full, mask, s_a, s_b, m_scr, l_scr, acc_scr):
    m_scr[...] = jnp.full(m_scr.shape, -jnp.inf, F32)
    l_scr[...] = jnp.zeros(l_scr.shape, F32)
    acc_scr[...] = jnp.zeros(acc_scr.shape, F32)

    def softmax_pv(s_ref, j, msk):
        st = s_ref[...]
        if msk is not None:
            st = jnp.where(msk, st, -jnp.inf)
        m_prev = m_scr[...]
        m_new = jnp.maximum(m_prev, jnp.max(st, axis=0, keepdims=True))
        pt = jnp.exp2(st - m_new)
        alpha = jnp.exp2(m_prev - m_new)
        l_scr[...] = alpha * l_scr[...] + jnp.sum(pt, axis=0, keepdims=True)
        acc_scr[...] = alpha * acc_scr[...] + jnp.dot(
            vt(j), pt.astype(BF16), preferred_element_type=F32)
        m_scr[...] = m_new

    s_a[...] = scores(0)

    def pair(jj, carry):
        j0 = 2 * jj
        s_b[...] = scores(j0 + 1)
        softmax_pv(s_a, j0, None)
        s_a[...] = scores(j0 + 2)
        softmax_pv(s_b, j0 + 1, None)
        return carry

    lax.fori_loop(0, n_full // 2, pair, 0)

    @pl.when(n_full % 2 == 1)
    def _():
        s_b[...] = scores(n_full)
        softmax_pv(s_a, n_full - 1, None)
        softmax_pv(s_b, n_full, mask)

    @pl.when(n_full % 2 == 0)
    def _():
        softmax_pv(s_a, n_full, mask)


def _flash_scratch(bk, nq, dv, nchunk):
    return [
        pltpu.VMEM((nchunk, dv, bk), BF16),
        pltpu.VMEM((bk, nq), F32),
        pltpu.VMEM((bk, nq), F32),
        pltpu.VMEM((1, nq), F32),
        pltpu.VMEM((1, nq), F32),
        pltpu.VMEM((dv, nq), F32),
    ]


def _store_vt(v_ref, vt_scr, bk):
    for c in range(vt_scr.shape[0]):
        vt_scr[c] = v_ref[c * bk:(c + 1) * bk, :].T


def _flash_t_static(nblk, scores, vt, mask, s_bufs, finalize):
    pairs = [(i, j) for i in range(nblk) for j in range(i + 1)]

    def produce(c):
        i, j = pairs[c]
        st = scores(i, j)
        if j == i:
            st = jnp.where(mask, st, -jnp.inf)
        s_bufs[c % 2][...] = st
        return jnp.max(st, axis=0, keepdims=True)

    cmax_next = produce(0)
    m = l = acc = None
    for c, (i, j) in enumerate(pairs):
        cmax = cmax_next
        if c + 1 < len(pairs):
            cmax_next = produce(c + 1)
        st = s_bufs[c % 2][...]
        if j == 0:
            m_new = cmax
            pt = jnp.exp2(st - m_new)
            l = jnp.sum(pt, axis=0, keepdims=True)
            acc = jnp.dot(vt(j), pt.astype(BF16), preferred_element_type=F32)
        else:
            m_new = jnp.maximum(m, cmax)
            pt = jnp.exp2(st - m_new)
            alpha = jnp.exp2(m - m_new)
            l = alpha * l + jnp.sum(pt, axis=0, keepdims=True)
            acc = alpha * acc + jnp.dot(vt(j), pt.astype(BF16), preferred_element_type=F32)
        m = m_new
        if j == i:
            finalize(i, l, acc)


def _mla_attn_body(q_ref, k_ref, v_ref, gate_ref, o_ref, qt_scr, vt_scr, s_a, s_b, *, bq):
    nblk = q_ref.shape[0] // bq
    for c in range(nblk):
        rows = slice(c * bq, (c + 1) * bq)
        vt_scr[:, rows] = v_ref[rows, :].T
        qt_scr[:, rows] = q_ref[rows, :].T

    def scores(i, j):
        return jnp.dot(k_ref[j * bq:(j + 1) * bq, :], qt_scr[:, i * bq:(i + 1) * bq],
                       preferred_element_type=F32)

    def finalize(i, l, acc):
        rows = slice(i * bq, (i + 1) * bq)
        g = gate_ref[rows, :].astype(F32)
        o_ref[rows, :] = ((acc / l).T * _silu(g)).astype(o_ref.dtype)

    key = lax.broadcasted_iota(jnp.int32, (bq, bq), 0)
    qry = lax.broadcasted_iota(jnp.int32, (bq, bq), 1)
    _flash_t_static(nblk, scores, lambda j: vt_scr[:, j * bq:(j + 1) * bq], key <= qry,
                    (s_a, s_b), finalize)


def _mla_attn(q, k, v, z1, *, batch, seq, bq):
    m = batch * seq
    return pl.pallas_call(
        functools.partial(_mla_attn_body, bq=bq),
        grid=(batch, MLA_HEADS),
        in_specs=[
            pl.BlockSpec((seq, MLA_QK_PAD), lambda b, h: (b, h)),
            pl.BlockSpec((seq, MLA_QK_PAD), lambda b, h: (b, h)),
            pl.BlockSpec((seq, MLA_V), lambda b, h: (b, h)),
            pl.BlockSpec((seq, MLA_V), lambda b, h: (b, Z1_MLA_GATE // MLA_V + h)),
        ],
        out_specs=pl.BlockSpec((seq, MLA_V), lambda b, h: (b, h)),
        out_shape=jax.ShapeDtypeStruct((m, MLA_WIDTH), BF16),
        scratch_shapes=[
            pltpu.VMEM((MLA_QK_PAD, seq), BF16),
            pltpu.VMEM((MLA_V, seq), BF16),
            pltpu.VMEM((bq, bq), F32),
            pltpu.VMEM((bq, bq), F32),
        ],
        compiler_params=_params("parallel", "parallel"),
        name="mla_attn",
    )(q, k, v, z1)


def _diff_attn_body(q_ref, k_ref, v_ref, gate_ref, lq1_ref, lk1_ref, lq2_ref, lk2_ref, sub_ref,
                    o_ref, qt_scr, vt_scr, s_a, s_b, *, bq):
    nblk = q_ref.shape[0] // bq
    lane = lax.broadcasted_iota(jnp.int32, (bq, DIFF_V), 1)
    for c in range(nblk):
        rows = slice(c * bq, (c + 1) * bq)
        vt_scr[:, rows] = v_ref[rows, :].T
        q = q_ref[rows, :].astype(F32) * (DIFF_QK ** -0.5 * math.log2(math.e))
        qt_scr[:, 2 * c * bq:(2 * c + 1) * bq] = jnp.where(lane < DIFF_QK, q, 0.0).astype(BF16).T
        qt_scr[:, (2 * c + 1) * bq:(2 * c + 2) * bq] = (
            jnp.where(lane >= DIFF_QK, q, 0.0).astype(BF16).T)

    lam = (jnp.exp(jnp.sum(lq1_ref[...] * lk1_ref[...], axis=-1, keepdims=True))
           - jnp.exp(jnp.sum(lq2_ref[...] * lk2_ref[...], axis=-1, keepdims=True))
           + DIFF_LAMBDA_INIT)

    def scores(i, j):
        return jnp.dot(k_ref[j * bq:(j + 1) * bq, :], qt_scr[:, 2 * i * bq:(2 * i + 2) * bq],
                       preferred_element_type=F32)

    def finalize(i, l, acc):
        rows = slice(i * bq, (i + 1) * bq)
        ot = acc / l
        od = (ot[:, :bq] - lam * ot[:, bq:]).T
        ms = jnp.mean(od * od, axis=-1, keepdims=True)
        od = od * lax.rsqrt(ms + NORM_EPS) * sub_ref[...] * (1.0 - DIFF_LAMBDA_INIT)
        g = gate_ref[rows, :].astype(F32)
        o_ref[rows, :] = (od * _silu(g)).astype(o_ref.dtype)

    key = lax.broadcasted_iota(jnp.int32, (bq, 2 * bq), 0)
    qry = lax.broadcasted_iota(jnp.int32, (bq, 2 * bq), 1)
    qry = jnp.where(qry >= bq, qry - bq, qry)
    _flash_t_static(nblk, scores, lambda j: vt_scr[:, j * bq:(j + 1) * bq], key <= qry,
                    (s_a, s_b), finalize)


def _diff_attn(z1, lq1, lk1, lq2, lk2, subln, *, batch, seq, bq):
    m = batch * seq
    vec = lambda n: pl.BlockSpec((1, n), lambda b, h: (0, 0))
    col = lambda c0: pl.BlockSpec((seq, DIFF_V), lambda b, h: (b, c0 // DIFF_V + h))
    return pl.pallas_call(
        functools.partial(_diff_attn_body, bq=bq),
        grid=(batch, DIFF_HEADS),
        in_specs=[
            col(Z1_DQ), col(Z1_DK), col(Z1_DV), col(Z1_DIFF_GATE),
            vec(DIFF_QK), vec(DIFF_QK), vec(DIFF_QK), vec(DIFF_QK), vec(DIFF_V),
        ],
        out_specs=pl.BlockSpec((seq, DIFF_V), lambda b, h: (b, h)),
        out_shape=jax.ShapeDtypeStruct((m, DIFF_WIDTH), BF16),
        scratch_shapes=[
            pltpu.VMEM((DIFF_V, 2 * seq), BF16),
            pltpu.VMEM((DIFF_V, seq), BF16),
            pltpu.VMEM((bq, 2 * bq), F32),
            pltpu.VMEM((bq, 2 * bq), F32),
        ],
        compiler_params=_params("parallel", "parallel"),
        name="diff_attn",
    )(z1, z1, z1, z1, lq1.reshape(1, -1), lk1.reshape(1, -1), lq2.reshape(1, -1),
      lk2.reshape(1, -1), subln.reshape(1, -1))


def _layer0_w_in(w):
    q0 = 2 * LRU_WIDTH
    k0 = q0 + SWA_WIDTH
    g0 = k0 + 2 * SWA_KV_WIDTH
    return jnp.concatenate([w[:, :k0], w[:, g0:], w[:, k0:g0]], axis=1).astype(BF16)


def _layer1_w_in(w):
    ckv0 = MLA_Q_RANK
    kr0 = ckv0 + MLA_KV_RANK
    rest0 = kr0 + MLA_ROPE
    pad = jnp.zeros((w.shape[0], Z1_CKV - Z1_KROPE - MLA_ROPE), w.dtype)
    return jnp.concatenate(
        [w[:, :ckv0], w[:, kr0:rest0], pad, w[:, ckv0:kr0], w[:, rest0:]], axis=1).astype(BF16)


def _pad_w_uq(w):
    w = w.reshape(MLA_Q_RANK, MLA_HEADS, MLA_NOPE + MLA_ROPE)
    w = jnp.pad(w, ((0, 0), (0, 0), (0, MLA_QK_PAD - MLA_NOPE - MLA_ROPE)))
    return w.reshape(MLA_Q_RANK, MLA_HEADS * MLA_QK_PAD).astype(BF16)


def _rope_tables(seq):
    half = MLA_ROPE // 2
    freq = ROPE_THETA ** (-jnp.arange(half, dtype=F32) / half)
    ang = jnp.arange(seq).astype(F32)[:, None] * freq[None, :]
    cos, sin = jnp.cos(ang), jnp.sin(ang)
    zeros = jnp.zeros((seq, LANES - MLA_ROPE), F32)
    return (jnp.concatenate([cos, cos, zeros], axis=1),
            jnp.concatenate([-sin, sin, zeros], axis=1))


def kernel(x, norm_gains, final_norm_gain, l0_w_in, l0_conv_w, l0_conv_b, l0_gate_x_w, l0_gate_x_b, l0_gate_a_w, l0_gate_a_b, l0_lru_lambda, l0_sinks, l0_w_out, l1_w_in, l1_q_norm, l1_w_uq, l1_kv_norm, l1_w_ukv, l1_lambda_q1, l1_lambda_k1, l1_lambda_q2, l1_lambda_k2, l1_subln, l1_w_out):
    batch, seq, d = x.shape
    m = batch * seq
    x0 = x.reshape(m, d)
    tm = min(512, seq)
    bq = min(512, seq)
    t_lru = min(256, seq)

    z0 = _norm_matmul(x0, norm_gains[0], _layer0_w_in(l0_w_in),
                      tm=tm, n_blocks=1, chunk=512, name="in_proj0")
    y_a = _lru(z0, l0_conv_w, l0_conv_b, l0_gate_x_w, l0_gate_x_b, l0_gate_a_w, l0_gate_a_b,
               l0_lru_lambda, batch=batch, seq=seq, t=t_lru)
    y_b = _swa(z0, l0_sinks, batch=batch, seq=seq)
    x1 = _out_proj(y_a, y_b, l0_w_out.astype(BF16), x0, final_norm_gain,
                   tm=tm, chunk=512, final_norm=False, name="out_proj0")

    z1 = _norm_matmul(x1, norm_gains[1], _layer1_w_in(l1_w_in),
                      tm=tm, n_blocks=2, chunk=512, name="in_proj1")
    cos2, sin2 = _rope_tables(seq)
    q, k, v = _mla_up(z1, l1_q_norm, l1_kv_norm, _pad_w_uq(l1_w_uq), l1_w_ukv.astype(BF16),
                      cos2, sin2, seq=seq, tm=tm)
    y_c = _mla_attn(q, k, v, z1, batch=batch, seq=seq, bq=bq)
    y_d = _diff_attn(z1, l1_lambda_q1, l1_lambda_k1, l1_lambda_q2, l1_lambda_k2, l1_subln,
                     batch=batch, seq=seq, bq=bq)
    out = _out_proj(y_c, y_d, l1_w_out.astype(BF16), x1, final_norm_gain,
                    tm=tm, chunk=512, final_norm=True, name="out_proj1")
    return out.reshape(batch, seq, d)
```

```python
import functools
import math

import jax
import jax.numpy as jnp
from jax import lax
from jax.experimental import pallas as pl
from jax.experimental.pallas import tpu as pltpu

F32 = jnp.float32
BF16 = jnp.bfloat16

D_MODEL = 2048
NORM_EPS = 1e-6
LANES = 128
SUBLANES = 8
VMEM_LIMIT = 56 * 1024 * 1024

ROW_TILE = 512
ATTN_BLOCK = 512
LRU_CHUNK = 256
SWA_SUBBLOCKS = 4

LRU_WIDTH = 1024
LRU_BLOCKS = 8
LRU_BLOCK_DIM = LRU_WIDTH // LRU_BLOCKS
CONV_WIDTH = 4
LRU_C = 8.0

SWA_Q_HEADS = 16
SWA_KV_HEADS = 2
SWA_HEAD_DIM = 64
SWA_GROUP = SWA_Q_HEADS // SWA_KV_HEADS
SWA_WIDTH = SWA_Q_HEADS * SWA_HEAD_DIM
SWA_KV_WIDTH = SWA_KV_HEADS * SWA_HEAD_DIM
SWA_BLOCK = 128

MLA_HEADS = 8
MLA_Q_RANK = 768
MLA_KV_RANK = 512
MLA_NOPE = 128
MLA_ROPE = 64
MLA_V = 128
MLA_QK_PAD = 256
MLA_WIDTH = MLA_HEADS * MLA_V
ROPE_THETA = 10000.0

DIFF_HEADS = 8
DIFF_QK = 64
DIFF_V = 2 * DIFF_QK
DIFF_WIDTH = DIFF_HEADS * DIFF_V
DIFF_LAMBDA_INIT = 0.8 - 0.6 * math.exp(-0.3 * 1)
LOG2E = math.log2(math.e)

Z1_CQ = 0
Z1_KROPE = 768
Z1_CKV = 1024
Z1_MLA_GATE = 1536
Z1_DQ = 2560
Z1_DK = 3584
Z1_DV = 4608
Z1_DIFF_GATE = 5632
Z1_WIDTH = 6656

Z0_LRU_X = 0
Z0_LRU_GATE = 1024
Z0_Q = 2048
Z0_SWA_GATE = 3072
Z0_K = 4096
Z0_V = 4224
Z0_WIDTH = 4352


def _params(*sem):
    return pltpu.CompilerParams(dimension_semantics=sem, vmem_limit_bytes=VMEM_LIMIT)


def _silu(g):
    return g * jax.nn.sigmoid(g)


def _col_chunks(n, width):
    out, c = [], 0
    while c < n:
        w = min(width, n - c)
        out.append((c, w))
        c += w
    return out


def _norm_matmul_body(x_ref, g_ref, w_ref, o_ref, *, chunk):
    x = x_ref[...]
    ms = jnp.mean(x * x, axis=-1, keepdims=True)
    h = (x * lax.rsqrt(ms + NORM_EPS) * g_ref[...]).astype(BF16)
    for c0, cw in _col_chunks(w_ref.shape[1], chunk):
        o_ref[:, c0:c0 + cw] = jnp.dot(
            h, w_ref[:, c0:c0 + cw], preferred_element_type=F32).astype(o_ref.dtype)


def _norm_matmul(x, g, w, *, tm, n_blocks, chunk, name):
    m, d = x.shape
    n = w.shape[1]
    tn = n // n_blocks
    w_mode = dict(pipeline_mode=pl.Buffered(1)) if n_blocks == 1 else {}
    return pl.pallas_call(
        functools.partial(_norm_matmul_body, chunk=chunk),
        grid=(n_blocks, m // tm),
        in_specs=[
            pl.BlockSpec((tm, d), lambda j, i: (i, 0)),
            pl.BlockSpec((1, d), lambda j, i: (0, 0)),
            pl.BlockSpec((d, tn), lambda j, i: (0, j), **w_mode),
        ],
        out_specs=pl.BlockSpec((tm, tn), lambda j, i: (i, j)),
        out_shape=jax.ShapeDtypeStruct((m, n), BF16),
        compiler_params=_params("parallel", "parallel"),
        name=name,
    )(x, g.reshape(1, d), w)


def _out_proj_body(ya_ref, yb_ref, w_ref, x_ref, g_ref, o_ref, *, chunk, final_norm):
    ya = ya_ref[...]
    yb = yb_ref[...]
    ka = ya.shape[1]
    ssq = None
    for c0, cw in _col_chunks(w_ref.shape[1], chunk):
        r = jnp.dot(ya, w_ref[:ka, c0:c0 + cw], preferred_element_type=F32)
        r = r + jnp.dot(yb, w_ref[ka:, c0:c0 + cw], preferred_element_type=F32)
        r = r + x_ref[:, c0:c0 + cw]
        o_ref[:, c0:c0 + cw] = r
        if final_norm:
            part = jnp.sum(r * r, axis=-1, keepdims=True)
            ssq = part if ssq is None else ssq + part
    if final_norm:
        inv = lax.rsqrt(ssq / w_ref.shape[1] + NORM_EPS)
        o_ref[...] = o_ref[...] * inv * g_ref[...]


def _out_proj(ya, yb, w, x, g, *, tm, chunk, final_norm, name):
    m, d = x.shape
    ka, kb = ya.shape[1], yb.shape[1]
    return pl.pallas_call(
        functools.partial(_out_proj_body, chunk=chunk, final_norm=final_norm),
        grid=(m // tm,),
        in_specs=[
            pl.BlockSpec((tm, ka), lambda i: (i, 0)),
            pl.BlockSpec((tm, kb), lambda i: (i, 0)),
            pl.BlockSpec((ka + kb, d), lambda i: (0, 0)),
            pl.BlockSpec((tm, d), lambda i: (i, 0)),
            pl.BlockSpec((1, d), lambda i: (0, 0)),
        ],
        out_specs=pl.BlockSpec((tm, d), lambda i: (i, 0)),
        out_shape=jax.ShapeDtypeStruct((m, d), F32),
        compiler_params=_params("parallel"),
        name=name,
    )(ya, yb, w, x, g.reshape(1, d))


def _lru_body(x_ref, gate_ref, cw_ref, cb_ref, gxw_ref, gxb_ref, gaw_ref, gab_ref, lam_ref,
              o_ref, tail_scr, hcar, a_scr, b_scr, *, t):
    c = pl.program_id(1)
    ng = t // SUBLANES
    shape3 = (ng, SUBLANES, LRU_WIDTH)

    @pl.when(c == 0)
    def _():
        tail_scr[...] = jnp.zeros((SUBLANES, LRU_WIDTH), F32)
        hcar[...] = jnp.zeros((SUBLANES, LRU_WIDTH), F32)

    row = lax.broadcasted_iota(jnp.int32, shape3, 1)
    x3 = x_ref[...].astype(F32).reshape(shape3)
    tail = tail_scr[...].reshape(1, SUBLANES, LRU_WIDTH)
    tail_scr[...] = x3[ng - 1]

    def delayed(k):
        r = pltpu.roll(x3, k, axis=1)
        r_prev = jnp.concatenate([pltpu.roll(tail, k, axis=1), r[:ng - 1]], axis=0)
        return jnp.where(row >= k, r, r_prev)

    cw = cw_ref[...]
    tap = lambda k: cw[k].reshape(1, 1, LRU_WIDTH)
    xc = cb_ref[...].reshape(1, 1, LRU_WIDTH) + tap(CONV_WIDTH - 1) * x3
    for k in range(1, CONV_WIDTH):
        xc = xc + tap(CONV_WIDTH - 1 - k) * delayed(k)
    xc = xc.reshape(t, LRU_WIDTH)

    xcb = xc.astype(BF16)
    gi_parts, gr_parts = [], []
    for n in range(LRU_BLOCKS):
        xb = xcb[:, n * LRU_BLOCK_DIM:(n + 1) * LRU_BLOCK_DIM]
        gi_parts.append(jnp.dot(xb, gxw_ref[n], preferred_element_type=F32))
        gr_parts.append(jnp.dot(xb, gaw_ref[n], preferred_element_type=F32))
    gate_i = jax.nn.sigmoid(jnp.concatenate(gi_parts, axis=-1) + gxb_ref[...])
    gate_r = jax.nn.sigmoid(jnp.concatenate(gr_parts, axis=-1) + gab_ref[...])

    neg_lam = -lam_ref[...]
    softplus = jnp.maximum(neg_lam, 0.0) + jnp.log1p(jnp.exp(-jnp.abs(neg_lam)))
    a = jnp.exp2((-LRU_C * LOG2E * softplus) * gate_r)
    u = jnp.sqrt(1.0 - a * a) * gate_i * xc

    a3 = a.reshape(shape3)
    u3 = u.reshape(shape3)
    d = 1
    while d < SUBLANES:
        ok = row >= d
        u3 = u3 + a3 * jnp.where(ok, pltpu.roll(u3, d, axis=1), 0.0)
        a3 = a3 * jnp.where(ok, pltpu.roll(a3, d, axis=1), 1.0)
        d *= 2
    a_scr[...] = a3.reshape(t, LRU_WIDTH)
    b_scr[...] = u3.reshape(t, LRU_WIDTH)

    def group(gidx, h_prev):
        r0 = pl.multiple_of(gidx * SUBLANES, SUBLANES)
        hg = a_scr[pl.ds(r0, SUBLANES), :] * h_prev + b_scr[pl.ds(r0, SUBLANES), :]
        b_scr[pl.ds(r0, SUBLANES), :] = hg
        return jnp.broadcast_to(hg[SUBLANES - 1:SUBLANES, :], (SUBLANES, LRU_WIDTH))

    hcar[...] = lax.fori_loop(0, t // SUBLANES, group, hcar[...])

    g = gate_ref[...].astype(F32)
    o_ref[...] = (b_scr[...] * _silu(g)).astype(o_ref.dtype)


def _lru(z0, conv_w, conv_b, gx_w, gx_b, ga_w, ga_b, lam, *, batch, seq, t):
    m = batch * seq
    nc = seq // t
    row_spec = lambda col: pl.BlockSpec((t, LRU_WIDTH), lambda b, c: (b * nc + c, col))
    full = lambda shape: pl.BlockSpec(shape, lambda b, c: (0,) * len(shape))
    return pl.pallas_call(
        functools.partial(_lru_body, t=t),
        grid=(batch, nc),
        in_specs=[
            row_spec(Z0_LRU_X // LRU_WIDTH),
            row_spec(Z0_LRU_GATE // LRU_WIDTH),
            full((CONV_WIDTH, LRU_WIDTH)),
            full((1, LRU_WIDTH)),
            full((LRU_BLOCKS, LRU_BLOCK_DIM, LRU_BLOCK_DIM)),
            full((1, LRU_WIDTH)),
            full((LRU_BLOCKS, LRU_BLOCK_DIM, LRU_BLOCK_DIM)),
            full((1, LRU_WIDTH)),
            full((1, LRU_WIDTH)),
        ],
        out_specs=pl.BlockSpec((t, LRU_WIDTH), lambda b, c: (b * nc + c, 0)),
        out_shape=jax.ShapeDtypeStruct((m, LRU_WIDTH), BF16),
        scratch_shapes=[
            pltpu.VMEM((SUBLANES, LRU_WIDTH), F32),
            pltpu.VMEM((SUBLANES, LRU_WIDTH), F32),
            pltpu.VMEM((t, LRU_WIDTH), F32),
            pltpu.VMEM((t, LRU_WIDTH), F32),
        ],
        compiler_params=_params("parallel", "arbitrary"),
        name="rg_lru",
    )(z0, z0, conv_w, conv_b.reshape(1, -1), gx_w.astype(BF16), gx_b.reshape(1, -1),
      ga_w.astype(BF16), ga_b.reshape(1, -1), lam.reshape(1, -1))


def _swa_body(sink_ref, q_ref, kp_ref, kc_ref, vp_ref, vc_ref, gate_ref, o_ref, *, nsub):
    step = pl.program_id(1)
    blk = SWA_BLOCK
    nq = SWA_GROUP * blk
    key = lax.broadcasted_iota(jnp.int32, (2 * blk, blk), 0)
    dist = key - lax.broadcasted_iota(jnp.int32, (2 * blk, blk), 1)
    zeros = jnp.zeros((SWA_HEAD_DIM, nq), BF16)

    def bias_for(first):
        valid = ((key < blk) & (dist > first)) | ((key >= blk) & (dist <= blk))
        bias1 = jnp.where(valid, 0.0, -jnp.inf).astype(F32)
        return jnp.concatenate([bias1] * SWA_GROUP, axis=1)

    bias_inner = bias_for(0)
    bias_first = bias_for(jnp.where(step > 0, 0, 2 * blk))

    for t in range(nsub):
        rows = slice(t * blk, (t + 1) * blk)
        prev = slice((t - 1) * blk, t * blk)
        bias = bias_first if t == 0 else bias_inner
        qs = (q_ref[rows, :].astype(F32) * (SWA_HEAD_DIM ** -0.5 * LOG2E)).astype(BF16)
        qt = qs.T
        kp = kp_ref[...] if t == 0 else kc_ref[prev, :]
        vp = vp_ref[...] if t == 0 else vc_ref[prev, :]
        k2 = jnp.concatenate([kp, kc_ref[rows, :]], axis=0)
        v2t = jnp.concatenate([vp, vc_ref[rows, :]], axis=0).T
        outs = []
        for g in range(SWA_KV_HEADS):
            qt_g = jnp.concatenate(
                [qt[(g * SWA_GROUP + hh) * SWA_HEAD_DIM:(g * SWA_GROUP + hh + 1) * SWA_HEAD_DIM, :]
                 for hh in range(SWA_GROUP)], axis=1)
            qt_gz = jnp.concatenate([qt_g, zeros] if g == 0 else [zeros, qt_g], axis=0)
            st = jnp.dot(k2, qt_gz, preferred_element_type=F32) + bias
            sink = sink_ref[g:g + 1, :] * LOG2E
            m = jnp.maximum(jnp.max(st, axis=0, keepdims=True), sink)
            pt = jnp.exp2(st - m)
            den = jnp.sum(pt, axis=0, keepdims=True) + jnp.exp2(sink - m)
            ot = jnp.dot(v2t, pt.astype(BF16), preferred_element_type=F32)
            ot = ot[g * SWA_HEAD_DIM:(g + 1) * SWA_HEAD_DIM, :] / den
            outs += [ot[:, hh * blk:(hh + 1) * blk] for hh in range(SWA_GROUP)]
        o = jnp.concatenate(outs, axis=0).T
        gt = gate_ref[rows, :].astype(F32)
        o_ref[rows, :] = (o * _silu(gt)).astype(o_ref.dtype)


def _swa(z0, sinks, *, batch, seq, nsub):
    m = batch * seq
    rows = nsub * SWA_BLOCK
    ns = seq // rows
    wide = lambda col: pl.BlockSpec((rows, SWA_WIDTH), lambda b, n: (b * ns + n, col))
    cur = lambda col: pl.BlockSpec((rows, SWA_KV_WIDTH), lambda b, n: (b * ns + n, col))
    prev = lambda col: pl.BlockSpec(
        (SWA_BLOCK, SWA_KV_WIDTH),
        lambda b, n: ((b * ns + n) * nsub - jnp.minimum(n, 1), col))
    sink_rows = jnp.repeat(sinks.reshape(SWA_KV_HEADS, SWA_GROUP), SWA_BLOCK, axis=1)
    return pl.pallas_call(
        functools.partial(_swa_body, nsub=nsub),
        grid=(batch, ns),
        in_specs=[
            pl.BlockSpec((SWA_KV_HEADS, SWA_GROUP * SWA_BLOCK), lambda b, n: (0, 0)),
            wide(Z0_Q // SWA_WIDTH),
            prev(Z0_K // SWA_KV_WIDTH), cur(Z0_K // SWA_KV_WIDTH),
            prev(Z0_V // SWA_KV_WIDTH), cur(Z0_V // SWA_KV_WIDTH),
            wide(Z0_SWA_GATE // SWA_WIDTH),
        ],
        out_specs=pl.BlockSpec((rows, SWA_WIDTH), lambda b, n: (b * ns + n, 0)),
        out_shape=jax.ShapeDtypeStruct((m, SWA_WIDTH), BF16),
        compiler_params=_params("parallel", "parallel"),
        name="swa",
    )(sink_rows, z0, z0, z0, z0, z0, z0)


def _rope128(x, cos2, sin2):
    lane = lax.broadcasted_iota(jnp.int32, x.shape, 1)
    swapped = jnp.where(lane < MLA_ROPE // 2,
                        pltpu.roll(x, LANES - MLA_ROPE // 2, axis=1),
                        pltpu.roll(x, MLA_ROPE // 2, axis=1))
    return x * cos2 + swapped * sin2


def _mla_up_body(cq_ref, ckv_ref, kr_ref, qn_ref, kvn_ref, wq_ref, wkv_ref, cos_ref, sin_ref,
                 q_ref, k_ref, v_ref):
    cos2 = cos_ref[...]
    sin2 = sin_ref[...]

    def normed(x, g):
        ms = jnp.mean(x * x, axis=-1, keepdims=True)
        return (x * lax.rsqrt(ms + NORM_EPS) * g).astype(BF16)

    hq = normed(cq_ref[:, :MLA_Q_RANK].astype(F32), qn_ref[...])
    hkv = normed(ckv_ref[...].astype(F32), kvn_ref[...])
    lane = lax.broadcasted_iota(jnp.int32, kr_ref.shape, 1)
    kr = jnp.where(lane < MLA_ROPE, kr_ref[...].astype(F32), 0.0)
    kr = _rope128(kr, cos2, sin2).astype(BF16)
    scale = (MLA_NOPE + MLA_ROPE) ** -0.5 * math.log2(math.e)
    for h in range(MLA_HEADS):
        c0 = h * MLA_QK_PAD
        qh = jnp.dot(hq, wq_ref[:, c0:c0 + MLA_QK_PAD], preferred_element_type=F32)
        q_ref[:, c0:c0 + MLA_NOPE] = (qh[:, :MLA_NOPE] * scale).astype(BF16)
        q_ref[:, c0 + MLA_NOPE:c0 + MLA_QK_PAD] = (
            _rope128(qh[:, MLA_NOPE:], cos2, sin2) * scale).astype(BF16)
        kvh = jnp.dot(hkv, wkv_ref[:, c0:c0 + MLA_NOPE + MLA_V], preferred_element_type=F32)
        k_ref[:, c0:c0 + MLA_NOPE] = kvh[:, :MLA_NOPE].astype(BF16)
        k_ref[:, c0 + MLA_NOPE:c0 + MLA_QK_PAD] = kr
        v_ref[:, h * MLA_V:(h + 1) * MLA_V] = kvh[:, MLA_NOPE:].astype(BF16)


def _mla_up(z1, q_norm, kv_norm, wq_pad, wkv, cos2, sin2, *, seq, tm):
    m = z1.shape[0]
    npos = seq // tm
    full = lambda shape: pl.BlockSpec(shape, lambda i: (0,) * len(shape))
    return pl.pallas_call(
        _mla_up_body,
        grid=(m // tm,),
        in_specs=[
            pl.BlockSpec((tm, 1024), lambda i: (i, Z1_CQ // 1024)),
            pl.BlockSpec((tm, MLA_KV_RANK), lambda i: (i, Z1_CKV // MLA_KV_RANK)),
            pl.BlockSpec((tm, LANES), lambda i: (i, Z1_KROPE // LANES)),
            full((1, MLA_Q_RANK)),
            full((1, MLA_KV_RANK)),
            full((MLA_Q_RANK, MLA_HEADS * MLA_QK_PAD)),
            full((MLA_KV_RANK, MLA_HEADS * (MLA_NOPE + MLA_V))),
            pl.BlockSpec((tm, LANES), lambda i: (i % npos, 0)),
            pl.BlockSpec((tm, LANES), lambda i: (i % npos, 0)),
        ],
        out_specs=[
            pl.BlockSpec((tm, MLA_HEADS * MLA_QK_PAD), lambda i: (i, 0)),
            pl.BlockSpec((tm, MLA_HEADS * MLA_QK_PAD), lambda i: (i, 0)),
            pl.BlockSpec((tm, MLA_WIDTH), lambda i: (i, 0)),
        ],
        out_shape=[
            jax.ShapeDtypeStruct((m, MLA_HEADS * MLA_QK_PAD), BF16),
            jax.ShapeDtypeStruct((m, MLA_HEADS * MLA_QK_PAD), BF16),
            jax.ShapeDtypeStruct((m, MLA_WIDTH), BF16),
        ],
        compiler_params=_params("parallel"),
        name="mla_up",
    )(z1, z1, z1, q_norm.reshape(1, -1), kv_norm.reshape(1, -1), wq_pad, wkv, cos2, sin2)


def _flash_t_static(nblk, scores, vt, mask, s_bufs, finalize):
    pairs = [(i, j) for i in range(nblk) for j in range(i + 1)]

    def produce(c):
        i, j = pairs[c]
        st = scores(i, j)
        if j == i:
            st = jnp.where(mask, st, -jnp.inf)
        s_bufs[c % 2][...] = st
        return jnp.max(st, axis=0, keepdims=True)

    cmax_next = produce(0)
    m = l = acc = None
    for c, (i, j) in enumerate(pairs):
        cmax = cmax_next
        if c + 1 < len(pairs):
            cmax_next = produce(c + 1)
        st = s_bufs[c % 2][...]
        if j == 0:
            m_new = cmax
            pt = jnp.exp2(st - m_new)
            l = jnp.sum(pt, axis=0, keepdims=True)
            acc = jnp.dot(vt(j), pt.astype(BF16), preferred_element_type=F32)
        else:
            m_new = jnp.maximum(m, cmax)
            pt = jnp.exp2(st - m_new)
            alpha = jnp.exp2(m - m_new)
            l = alpha * l + jnp.sum(pt, axis=0, keepdims=True)
            acc = alpha * acc + jnp.dot(vt(j), pt.astype(BF16), preferred_element_type=F32)
        m = m_new
        if j == i:
            finalize(i, l, acc)


def _mla_attn_body(q_ref, k_ref, v_ref, gate_ref, o_ref, qt_scr, vt_scr, s_a, s_b, *, bq):
    nblk = q_ref.shape[0] // bq
    for c in range(nblk):
        rows = slice(c * bq, (c + 1) * bq)
        vt_scr[:, rows] = v_ref[rows, :].T
        qt_scr[:, rows] = q_ref[rows, :].T

    def scores(i, j):
        return jnp.dot(k_ref[j * bq:(j + 1) * bq, :], qt_scr[:, i * bq:(i + 1) * bq],
                       preferred_element_type=F32)

    def finalize(i, l, acc):
        rows = slice(i * bq, (i + 1) * bq)
        g = gate_ref[rows, :].astype(F32)
        o_ref[rows, :] = ((acc / l).T * _silu(g)).astype(o_ref.dtype)

    key = lax.broadcasted_iota(jnp.int32, (bq, bq), 0)
    qry = lax.broadcasted_iota(jnp.int32, (bq, bq), 1)
    _flash_t_static(nblk, scores, lambda j: vt_scr[:, j * bq:(j + 1) * bq], key <= qry,
                    (s_a, s_b), finalize)


def _mla_attn(q, k, v, z1, *, batch, seq, bq):
    m = batch * seq
    return pl.pallas_call(
        functools.partial(_mla_attn_body, bq=bq),
        grid=(batch, MLA_HEADS),
        in_specs=[
            pl.BlockSpec((seq, MLA_QK_PAD), lambda b, h: (b, h)),
            pl.BlockSpec((seq, MLA_QK_PAD), lambda b, h: (b, h)),
            pl.BlockSpec((seq, MLA_V), lambda b, h: (b, h)),
            pl.BlockSpec((seq, MLA_V), lambda b, h: (b, Z1_MLA_GATE // MLA_V + h)),
        ],
        out_specs=pl.BlockSpec((seq, MLA_V), lambda b, h: (b, h)),
        out_shape=jax.ShapeDtypeStruct((m, MLA_WIDTH), BF16),
        scratch_shapes=[
            pltpu.VMEM((MLA_QK_PAD, seq), BF16),
            pltpu.VMEM((MLA_V, seq), BF16),
            pltpu.VMEM((bq, bq), F32),
            pltpu.VMEM((bq, bq), F32),
        ],
        compiler_params=_params("parallel", "parallel"),
        name="mla_attn",
    )(q, k, v, z1)


def _diff_attn_body(q_ref, k_ref, v_ref, gate_ref, lq1_ref, lk1_ref, lq2_ref, lk2_ref, sub_ref,
                    o_ref, qt_scr, vt_scr, s_a, s_b, *, bq):
    nblk = q_ref.shape[0] // bq
    lane = lax.broadcasted_iota(jnp.int32, (bq, DIFF_V), 1)
    for c in range(nblk):
        rows = slice(c * bq, (c + 1) * bq)
        vt_scr[:, rows] = v_ref[rows, :].T
        q = q_ref[rows, :].astype(F32) * (DIFF_QK ** -0.5 * math.log2(math.e))
        qt_scr[:, 2 * c * bq:(2 * c + 1) * bq] = jnp.where(lane < DIFF_QK, q, 0.0).astype(BF16).T
        qt_scr[:, (2 * c + 1) * bq:(2 * c + 2) * bq] = (
            jnp.where(lane >= DIFF_QK, q, 0.0).astype(BF16).T)

    lam = (jnp.exp(jnp.sum(lq1_ref[...] * lk1_ref[...], axis=-1, keepdims=True))
           - jnp.exp(jnp.sum(lq2_ref[...] * lk2_ref[...], axis=-1, keepdims=True))
           + DIFF_LAMBDA_INIT)

    def scores(i, j):
        return jnp.dot(k_ref[j * bq:(j + 1) * bq, :], qt_scr[:, 2 * i * bq:(2 * i + 2) * bq],
                       preferred_element_type=F32)

    def finalize(i, l, acc):
        rows = slice(i * bq, (i + 1) * bq)
        ot = acc / l
        od = (ot[:, :bq] - lam * ot[:, bq:]).T
        ms = jnp.mean(od * od, axis=-1, keepdims=True)
        od = od * lax.rsqrt(ms + NORM_EPS) * sub_ref[...] * (1.0 - DIFF_LAMBDA_INIT)
        g = gate_ref[rows, :].astype(F32)
        o_ref[rows, :] = (od * _silu(g)).astype(o_ref.dtype)

    key = lax.broadcasted_iota(jnp.int32, (bq, 2 * bq), 0)
    qry = lax.broadcasted_iota(jnp.int32, (bq, 2 * bq), 1)
    qry = jnp.where(qry >= bq, qry - bq, qry)
    _flash_t_static(nblk, scores, lambda j: vt_scr[:, j * bq:(j + 1) * bq], key <= qry,
                    (s_a, s_b), finalize)


def _diff_attn(z1, lq1, lk1, lq2, lk2, subln, *, batch, seq, bq):
    m = batch * seq
    vec = lambda n: pl.BlockSpec((1, n), lambda b, h: (0, 0))
    col = lambda c0: pl.BlockSpec((seq, DIFF_V), lambda b, h: (b, c0 // DIFF_V + h))
    return pl.pallas_call(
        functools.partial(_diff_attn_body, bq=bq),
        grid=(batch, DIFF_HEADS),
        in_specs=[
            col(Z1_DQ), col(Z1_DK), col(Z1_DV), col(Z1_DIFF_GATE),
            vec(DIFF_QK), vec(DIFF_QK), vec(DIFF_QK), vec(DIFF_QK), vec(DIFF_V),
        ],
        out_specs=pl.BlockSpec((seq, DIFF_V), lambda b, h: (b, h)),
        out_shape=jax.ShapeDtypeStruct((m, DIFF_WIDTH), BF16),
        scratch_shapes=[
            pltpu.VMEM((DIFF_V, 2 * seq), BF16),
            pltpu.VMEM((DIFF_V, seq), BF16),
            pltpu.VMEM((bq, 2 * bq), F32),
            pltpu.VMEM((bq, 2 * bq), F32),
        ],
        compiler_params=_params("parallel", "parallel"),
        name="diff_attn",
    )(z1, z1, z1, z1, lq1.reshape(1, -1), lk1.reshape(1, -1), lq2.reshape(1, -1),
      lk2.reshape(1, -1), subln.reshape(1, -1))


def _layer0_w_in(w):
    q0 = 2 * LRU_WIDTH
    k0 = q0 + SWA_WIDTH
    g0 = k0 + 2 * SWA_KV_WIDTH
    w = w.astype(BF16)
    return jnp.concatenate([w[:, :k0], w[:, g0:], w[:, k0:g0]], axis=1)


def _layer1_w_in(w):
    ckv0 = MLA_Q_RANK
    kr0 = ckv0 + MLA_KV_RANK
    rest0 = kr0 + MLA_ROPE
    w = w.astype(BF16)
    pad = jnp.zeros((w.shape[0], Z1_CKV - Z1_KROPE - MLA_ROPE), BF16)
    return jnp.concatenate(
        [w[:, :ckv0], w[:, kr0:rest0], pad, w[:, ckv0:kr0], w[:, rest0:]], axis=1)


def _pad_w_uq(w):
    w = w.astype(BF16).reshape(MLA_Q_RANK, MLA_HEADS, MLA_NOPE + MLA_ROPE)
    w = jnp.pad(w, ((0, 0), (0, 0), (0, MLA_QK_PAD - MLA_NOPE - MLA_ROPE)))
    return w.reshape(MLA_Q_RANK, MLA_HEADS * MLA_QK_PAD)


def _rope_tables(seq):
    half = MLA_ROPE // 2
    freq = ROPE_THETA ** (-jnp.arange(half, dtype=F32) / half)
    ang = jnp.arange(seq).astype(F32)[:, None] * freq[None, :]
    cos, sin = jnp.cos(ang), jnp.sin(ang)
    zeros = jnp.zeros((seq, LANES - MLA_ROPE), F32)
    return (jnp.concatenate([cos, cos, zeros], axis=1),
            jnp.concatenate([-sin, sin, zeros], axis=1))


def kernel(x, norm_gains, final_norm_gain, l0_w_in, l0_conv_w, l0_conv_b, l0_gate_x_w, l0_gate_x_b, l0_gate_a_w, l0_gate_a_b, l0_lru_lambda, l0_sinks, l0_w_out, l1_w_in, l1_q_norm, l1_w_uq, l1_kv_norm, l1_w_ukv, l1_lambda_q1, l1_lambda_k1, l1_lambda_q2, l1_lambda_k2, l1_subln, l1_w_out):
    batch, seq, d = x.shape
    m = batch * seq
    x0 = x.reshape(m, d)
    tm = min(ROW_TILE, seq)
    bq = min(ATTN_BLOCK, seq)
    t_lru = min(LRU_CHUNK, seq)
    nsub = min(SWA_SUBBLOCKS, seq // SWA_BLOCK)

    z0 = _norm_matmul(x0, norm_gains[0], _layer0_w_in(l0_w_in),
                      tm=tm, n_blocks=1, chunk=512, name="in_proj0")
    y_a = _lru(z0, l0_conv_w, l0_conv_b, l0_gate_x_w, l0_gate_x_b, l0_gate_a_w, l0_gate_a_b,
               l0_lru_lambda, batch=batch, seq=seq, t=t_lru)
    y_b = _swa(z0, l0_sinks, batch=batch, seq=seq, nsub=nsub)
    x1 = _out_proj(y_a, y_b, l0_w_out.astype(BF16), x0, final_norm_gain,
                   tm=tm, chunk=512, final_norm=False, name="out_proj0")

    z1 = _norm_matmul(x1, norm_gains[1], _layer1_w_in(l1_w_in),
                      tm=tm, n_blocks=2, chunk=512, name="in_proj1")
    cos2, sin2 = _rope_tables(seq)
    q, k, v = _mla_up(z1, l1_q_norm, l1_kv_norm, _pad_w_uq(l1_w_uq), l1_w_ukv.astype(BF16),
                      cos2, sin2, seq=seq, tm=tm)
    y_c = _mla_attn(q, k, v, z1, batch=batch, seq=seq, bq=bq)
    y_d = _diff_attn(z1, l1_lambda_q1, l1_lambda_k1, l1_lambda_q2, l1_lambda_k2, l1_subln,
                     batch=batch, seq=seq, bq=bq)
    out = _out_proj(y_c, y_d, l1_w_out.astype(BF16), x1, final_norm_gain,
                    tm=tm, chunk=512, final_norm=True, name="out_proj1")
    return out.reshape(batch, seq, d)
```
